```python
import jax, jax.numpy as jnp
from jax import lax
import numpy as np

D_MODEL = 1024
BATCH = 8
SEQ = 2048
DEPTH = 1
DEC_BATCH = 32
DEC_SEQ = 4
PAST_LEN = 16384
PAGE_SIZE = 128

C_ATTN = D_MODEL // 2
C_CONV = D_MODEL - C_ATTN
HEAD_DIM = 64
N_HEADS = C_ATTN // HEAD_DIM
CONV_W = 31
D_FF = -(-8 * D_MODEL // (3 * 256)) * 256
BLOCK_Q = 128
ALPHA = (2 * DEPTH) ** 0.25
DEEPNORM_BETA = (8 * DEPTH) ** -0.25
LN_EPS = 1e-5
D_IN = 3 * C_ATTN + 2 * C_CONV

kernel_name = "hymba_stickbreak_conformer_decoder_step"


def _layer_norm(x, g=None, b=None):
    xf = x.astype(jnp.float32)
    mu = jnp.mean(xf, axis=-1, keepdims=True)
    var = jnp.mean(jnp.square(xf - mu), axis=-1, keepdims=True)
    y = (xf - mu) * lax.rsqrt(var + LN_EPS)
    if g is not None:
        y = y * g.astype(jnp.float32) + b.astype(jnp.float32)
    return y.astype(x.dtype)


def _stick_breaking_block(q, k, v, sb_bias, q_pos, k_pos):
    z = jnp.einsum('bqhd,bkhd->bhqk', q, k, preferred_element_type=jnp.float32) * (HEAD_DIM ** -0.5)
    z = z + sb_bias.astype(jnp.float32)[None, :, None, None]
    mask = k_pos[None, :] < q_pos[:, None]
    log_one_minus = jnp.where(mask, jax.nn.log_sigmoid(-z), 0.0)
    between = lax.cumsum(log_one_minus, axis=3, reverse=True) - log_one_minus
    w = jnp.where(mask, jnp.exp(jax.nn.log_sigmoid(z) + between), 0.0)
    return jnp.einsum('bhqk,bkhd->bqhd', w.astype(v.dtype), v)


def _stick_breaking_attention(q, k_all, v_all, sb_bias, q_offset):
    T = q.shape[1]
    outs = []
    for start in range(0, T, BLOCK_Q):
        end = min(start + BLOCK_Q, T)
        n_keys = q_offset + end
        q_pos = jnp.arange(q_offset + start, q_offset + end)
        k_pos = jnp.arange(n_keys)
        outs.append(_stick_breaking_block(q[:, start:end], k_all[:, :n_keys], v_all[:, :n_keys],
                                          sb_bias, q_pos, k_pos))
    return jnp.concatenate(outs, axis=1)


def _depthwise_causal_conv(u_ext, conv_w, conv_b):
    out = lax.conv_general_dilated(u_ext, conv_w[:, None, :].astype(u_ext.dtype), (1,), 'VALID',
                                   dimension_numbers=('NWC', 'WIO', 'NWC'),
                                   feature_group_count=C_CONV)
    return out + conv_b


def _layer(x, c, k_past, v_past, conv_buf, q_offset, w_ada, b_ada, w_in, sb_bias, conv_w, conv_b,
           cln_g, cln_b, w_o, ln1_g, ln1_b, w_gu, w_down, ln2_g, ln2_b):
    B, T = x.shape[0], x.shape[1]
    mod = (jax.nn.silu(c) @ w_ada + b_ada)[:, None, :]
    shift1, scale1, gate1, shift2, scale2, gate2 = jnp.split(mod, 6, axis=-1)
    h = _layer_norm(x) * (1 + scale1) + shift1
    proj = h @ w_in
    q, k, v, a, g = jnp.split(proj, [C_ATTN, 2 * C_ATTN, 3 * C_ATTN, 3 * C_ATTN + C_CONV], axis=-1)
    q = q.reshape(B, T, N_HEADS, HEAD_DIM)
    k = k.reshape(B, T, N_HEADS, HEAD_DIM)
    v = v.reshape(B, T, N_HEADS, HEAD_DIM)
    k_all = k if k_past is None else jnp.concatenate([k_past, k], axis=1)
    v_all = v if v_past is None else jnp.concatenate([v_past, v], axis=1)
    attn = _stick_breaking_attention(q, k_all, v_all, sb_bias, q_offset).reshape(B, T, C_ATTN)
    u = a * jax.nn.sigmoid(g)
    u_ext = jnp.concatenate([conv_buf, u], axis=1)
    s = jax.nn.silu(_layer_norm(_depthwise_causal_conv(u_ext, conv_w, conv_b), cln_g, cln_b))
    mix = jnp.concatenate([attn, s], axis=-1) @ w_o
    x = _layer_norm(ALPHA * x + (1 + gate1) * mix, ln1_g, ln1_b)
    h2 = _layer_norm(x) * (1 + scale2) + shift2
    gt, up = jnp.split(h2 @ w_gu, 2, axis=-1)
    f = (jax.nn.silu(gt) * up) @ w_down
    x = _layer_norm(ALPHA * x + (1 + gate2) * f, ln2_g, ln2_b)
    return x, k, v, u_ext[:, -(CONV_W - 1):]


def setup_inputs(seed: int = 0) -> dict:
    key = jax.random.key(seed)
    ks = jax.random.split(key, 24)
    f32 = jnp.float32
    n_pages = PAST_LEN // PAGE_SIZE
    n_used = DEC_BATCH * n_pages
    n_pool = n_used + n_used // 4
    nrm = lambda k, shape, s: jax.random.normal(k, shape, f32) * s
    w_o_scale = DEEPNORM_BETA * D_MODEL ** -0.5
    sb_bias = jnp.broadcast_to(-jnp.linspace(2.0, 10.0, N_HEADS, dtype=f32), (DEPTH, N_HEADS)) \
        + nrm(ks[22], (DEPTH, N_HEADS), 0.1)
    return {
        'x_prompt': nrm(ks[0], (BATCH, SEQ, D_MODEL), 1.0),
        'x_sample': nrm(ks[1], (DEC_BATCH, DEC_SEQ, D_MODEL), 1.0),
        'c_prompt': nrm(ks[2], (BATCH, D_MODEL), 1.0),
        'c_sample': nrm(ks[3], (DEC_BATCH, D_MODEL), 1.0),
        'cache_k': nrm(ks[4], (DEPTH, n_pool, PAGE_SIZE, N_HEADS, HEAD_DIM), 1.0),
        'cache_v': nrm(ks[5], (DEPTH, n_pool, PAGE_SIZE, N_HEADS, HEAD_DIM), 1.0),
        'page_table': jax.random.permutation(ks[6], n_pool)[:n_used].reshape(DEC_BATCH, n_pages).astype(jnp.int32),
        'state_conv': nrm(ks[7], (DEPTH, DEC_BATCH, CONV_W - 1, C_CONV), 1.0),
        'w_ada': nrm(ks[8], (DEPTH, D_MODEL, 6 * D_MODEL), 0.5 * D_MODEL ** -0.5),
        'b_ada': nrm(ks[9], (DEPTH, 6 * D_MODEL), 0.02),
        'w_in': nrm(ks[10], (DEPTH, D_MODEL, D_IN), D_MODEL ** -0.5),
        'sb_bias': sb_bias,
        'conv_w': nrm(ks[11], (DEPTH, CONV_W, C_CONV), CONV_W ** -0.5),
        'conv_b': nrm(ks[12], (DEPTH, C_CONV), 0.02),
        'cln_g': 1.0 + nrm(ks[13], (DEPTH, C_CONV), 0.02),
        'cln_b': nrm(ks[14], (DEPTH, C_CONV), 0.02),
        'w_o': nrm(ks[15], (DEPTH, D_MODEL, D_MODEL), w_o_scale),
        'ln1_g': 1.0 + nrm(ks[16], (DEPTH, D_MODEL), 0.02),
        'ln1_b': nrm(ks[17], (DEPTH, D_MODEL), 0.02),
        'w_gu': nrm(ks[18], (DEPTH, D_MODEL, 2 * D_FF), D_MODEL ** -0.5),
        'w_down': nrm(ks[19], (DEPTH, D_FF, D_MODEL), DEEPNORM_BETA * D_FF ** -0.5),
        'ln2_g': 1.0 + nrm(ks[20], (DEPTH, D_MODEL), 0.02),
        'ln2_b': nrm(ks[21], (DEPTH, D_MODEL), 0.02),
    }


def reference(x_prompt, x_sample, c_prompt, c_sample, cache_k, cache_v, page_table, state_conv,
              w_ada, b_ada, w_in, sb_bias, conv_w, conv_b, cln_g, cln_b, w_o, ln1_g, ln1_b,
              w_gu, w_down, ln2_g, ln2_b):
    n_pages = PAST_LEN // PAGE_SIZE
    dec_b = x_sample.shape[0]
    yp, ys = x_prompt, x_sample
    kp_l, vp_l, cp_l, ks_l, vs_l, cs_l = [], [], [], [], [], []
    for l in range(DEPTH):
        params = (w_ada[l], b_ada[l], w_in[l], sb_bias[l], conv_w[l], conv_b[l], cln_g[l], cln_b[l],
                  w_o[l], ln1_g[l], ln1_b[l], w_gu[l], w_down[l], ln2_g[l], ln2_b[l])
        zero_buf = jnp.zeros((yp.shape[0], CONV_W - 1, C_CONV), yp.dtype)
        yp, kp, vp, cp = _layer(yp, c_prompt, None, None, zero_buf, 0, *params)
        k_past = cache_k[l][page_table].reshape(dec_b, n_pages * PAGE_SIZE, N_HEADS, HEAD_DIM)
        v_past = cache_v[l][page_table].reshape(dec_b, n_pages * PAGE_SIZE, N_HEADS, HEAD_DIM)
        ys, ksm, vsm, csm = _layer(ys, c_sample, k_past, v_past, state_conv[l], PAST_LEN, *params)
        kp_l.append(kp); vp_l.append(vp); cp_l.append(cp)
        ks_l.append(ksm); vs_l.append(vsm); cs_l.append(csm)
    new_k_prompt = jnp.stack(kp_l)
    new_v_prompt = jnp.stack(vp_l)
    new_conv_prompt = jnp.stack(cp_l)
    new_k_sample = jnp.stack(ks_l)
    new_v_sample = jnp.stack(vs_l)
    new_conv_sample = jnp.stack(cs_l)
    return (yp, ys, new_k_prompt, new_v_prompt, new_conv_prompt, new_k_sample, new_v_sample, new_conv_sample)
```

```python
import functools

import jax
import jax.numpy as jnp
from jax import lax
from jax.experimental import pallas as pl
from jax.experimental.pallas import tpu as pltpu

F32 = jnp.float32
BF16 = jnp.bfloat16

HEAD_DIM = 64
N_HEADS = 8
C_ATTN = N_HEADS * HEAD_DIM
CONV_W = 31
HALO = 32
LN_EPS = 1e-5
ALPHA = 2.0 ** 0.25
KEY_BLOCK = 128
VMEM_LIMIT = 56 * 1024 * 1024


def _ln(x):
    mu = jnp.mean(x, axis=-1, keepdims=True)
    xc = x - mu
    var = jnp.mean(xc * xc, axis=-1, keepdims=True)
    return xc * lax.rsqrt(var + LN_EPS)


def _silu(x):
    return x * jax.nn.sigmoid(x)


def _log_one_minus_beta(z):
    return -(jnp.maximum(z, 0.0) + jnp.log(1.0 + jnp.exp(-jnp.abs(z))))


def _suffix_matrix():
    r = lax.broadcasted_iota(jnp.int32, (2 * KEY_BLOCK, 2 * KEY_BLOCK), 0) % KEY_BLOCK
    c = lax.broadcasted_iota(jnp.int32, (2 * KEY_BLOCK, 2 * KEY_BLOCK), 1)
    return jnp.where((c >= KEY_BLOCK) | (r > c), 1.0, 0.0).astype(BF16)


def _suffix_sums(lom, m2):
    hi = lom.astype(BF16)
    lo = (lom - hi.astype(F32)).astype(BF16)
    cs = jnp.dot(jnp.concatenate([hi, lo], axis=1), m2, preferred_element_type=F32)
    return cs[:, :KEY_BLOCK], cs[:, KEY_BLOCK:]


def _ada_kernel(c_ref, w_ref, b_ref, o_ref):
    h = _silu(c_ref[...])
    o_ref[...] = jnp.dot(h, w_ref[...], preferred_element_type=F32) + b_ref[...]


def _ada(c, w_ada, b_ada):
    n, d = c.shape
    dn = w_ada.shape[1]
    tn = 1024
    return pl.pallas_call(
        _ada_kernel,
        grid=(dn // tn,),
        in_specs=[pl.BlockSpec((n, d), lambda j: (0, 0)),
                  pl.BlockSpec((d, tn), lambda j: (0, j)),
                  pl.BlockSpec((1, tn), lambda j: (0, j))],
        out_specs=pl.BlockSpec((n, tn), lambda j: (0, j)),
        out_shape=jax.ShapeDtypeStruct((n, dn), F32),
        compiler_params=pltpu.CompilerParams(dimension_semantics=("arbitrary",),
                                             vmem_limit_bytes=VMEM_LIMIT),
        name="ada",
    )(c, w_ada, b_ada.reshape(1, dn))


def _inproj_kernel(x_ref, shift_ref, scale_ref, w_ref, q_ref, k_ref, v_ref, kb_ref, vb_ref, u_ref):
    c = C_ATTN
    h = (_ln(x_ref[...]) * (1.0 + scale_ref[...]) + shift_ref[...]).astype(BF16)
    q = jnp.dot(h, w_ref[:, 0:c], preferred_element_type=F32)
    q_ref[...] = (q * (HEAD_DIM ** -0.5)).astype(BF16)
    k = jnp.dot(h, w_ref[:, c:2 * c], preferred_element_type=F32)
    k_ref[...] = k
    kb_ref[...] = k.astype(BF16)
    v = jnp.dot(h, w_ref[:, 2 * c:3 * c], preferred_element_type=F32)
    v_ref[...] = v
    vb_ref[...] = v.astype(BF16)
    cc = (w_ref.shape[1] - 3 * c) // 2
    a = jnp.dot(h, w_ref[:, 3 * c:3 * c + cc], preferred_element_type=F32)
    g = jnp.dot(h, w_ref[:, 3 * c + cc:], preferred_element_type=F32)
    u_ref[...] = a * jax.nn.sigmoid(g)


def _mod_spec(mod, tm, rows_per_group, chunk):
    d = mod.shape[-1] // 6
    if mod.shape[1] == 1:
        return pl.BlockSpec((None, 1, d), lambda i: ((i * tm) // rows_per_group, 0, chunk))
    return pl.BlockSpec((None, tm, d), lambda i: (0, i, chunk))


def _const_spec(shape):
    return pl.BlockSpec(shape, lambda *_: (0,) * len(shape), pipeline_mode=pl.Buffered(1))


def _inproj(x, mod, rows_per_group, w_in, tm):
    n, d = x.shape
    d_in = w_in.shape[1]
    cc = (d_in - 3 * C_ATTN) // 2
    tok = lambda w: pl.BlockSpec((tm, w), lambda i: (i, 0))
    sds = jax.ShapeDtypeStruct
    return pl.pallas_call(
        _inproj_kernel,
        grid=(n // tm,),
        in_specs=[tok(d), _mod_spec(mod, tm, rows_per_group, 0), _mod_spec(mod, tm, rows_per_group, 1),
                  _const_spec((d, d_in))],
        out_specs=[tok(C_ATTN)] * 5 + [tok(cc)],
        out_shape=[sds((n, C_ATTN), BF16), sds((n, C_ATTN), F32), sds((n, C_ATTN), F32),
                   sds((n, C_ATTN), BF16), sds((n, C_ATTN), BF16), sds((n, cc), F32)],
        compiler_params=pltpu.CompilerParams(dimension_semantics=("parallel",),
                                             vmem_limit_bytes=VMEM_LIMIT),
        name="inproj",
    )(x, mod, mod, w_in)


def _prompt_attn_kernel(bias_ref, q_ref, k_ref, v_ref, m2_ref, o_ref, acc_ref, run_ref):
    hp = pl.program_id(1)
    i = pl.program_id(2)
    tq = q_ref.shape[1]
    q = q_ref[0]
    lane = lax.broadcasted_iota(jnp.int32, (tq, 2 * HEAD_DIM), 1)
    row = lax.broadcasted_iota(jnp.int32, (tq, KEY_BLOCK), 0)
    col = lax.broadcasted_iota(jnp.int32, (tq, KEY_BLOCK), 1)
    causal = col < row
    m2 = m2_ref[...]
    heads = []
    for hh in range(2):
        in_head = (lane >= hh * HEAD_DIM) & (lane < (hh + 1) * HEAD_DIM)
        heads.append((jnp.where(in_head, q, jnp.zeros_like(q)), bias_ref[2 * hp + hh]))

    def block(j, mask):
        start = pl.multiple_of(j * KEY_BLOCK, KEY_BLOCK)
        kb = k_ref[0, pl.ds(start, KEY_BLOCK), :]
        vb = v_ref[0, pl.ds(start, KEY_BLOCK), :]
        for hh, (qm, bias) in enumerate(heads):
            z = lax.dot_general(qm, kb, (((1,), (1,)), ((), ())), preferred_element_type=F32) + bias
            lom = _log_one_minus_beta(z)
            if mask is not None:
                lom = jnp.where(mask, lom, 0.0)
            later, total = _suffix_sums(lom, m2)
            w = jnp.exp(lom + z + later + run_ref[hh])
            if mask is not None:
                w = jnp.where(mask, w, 0.0)
            acc_ref[hh] += jnp.dot(w.astype(BF16), vb, preferred_element_type=F32)
            run_ref[hh] += total

    acc_ref[...] = jnp.zeros_like(acc_ref)
    run_ref[...] = jnp.zeros_like(run_ref)
    block(i, causal)

    def body(t, carry):
        block(i - 1 - t, None)
        return carry

    lax.fori_loop(0, i, body, 0)
    o_ref[0] = jnp.where(lane < HEAD_DIM, acc_ref[0], acc_ref[1]).astype(o_ref.dtype)


def _prompt_attn(q, k, v, sb_bias, m2):
    b, t, c = q.shape
    tq = KEY_BLOCK
    hw = 2 * HEAD_DIM
    grid_spec = pltpu.PrefetchScalarGridSpec(
        num_scalar_prefetch=1,
        grid=(b, c // hw, t // tq),
        in_specs=[pl.BlockSpec((1, tq, hw), lambda bb, hp, i, bias: (bb, i, hp)),
                  pl.BlockSpec((1, t, hw), lambda bb, hp, i, bias: (bb, 0, hp)),
                  pl.BlockSpec((1, t, hw), lambda bb, hp, i, bias: (bb, 0, hp)),
                  pl.BlockSpec(m2.shape, lambda bb, hp, i, bias: (0, 0))],
        out_specs=pl.BlockSpec((1, tq, hw), lambda bb, hp, i, bias: (bb, i, hp)),
        scratch_shapes=[pltpu.VMEM((2, tq, hw), F32), pltpu.VMEM((2, tq, KEY_BLOCK), F32)],
    )
    return pl.pallas_call(
        _prompt_attn_kernel,
        grid_spec=grid_spec,
        out_shape=jax.ShapeDtypeStruct((b, t, c), BF16),
        compiler_params=pltpu.CompilerParams(
            dimension_semantics=("parallel", "parallel", "arbitrary"), vmem_limit_bytes=VMEM_LIMIT),
        name="prompt_attn",
    )(sb_bias, q, k, v, m2)


def _sample_attn_kernel(pages_per_step, pt_ref, q_ref, bias_ref, knew_ref, vnew_ref, m2_ref, *refs):
    k_refs = refs[:pages_per_step]
    v_refs = refs[pages_per_step:2 * pages_per_step]
    o_ref, acc_ref, run_ref = refs[2 * pages_per_step:]
    s = pl.program_id(1)
    nq = q_ref.shape[1]
    rows = nq * N_HEADS
    c = q_ref.shape[2]
    row = lax.broadcasted_iota(jnp.int32, (rows, c), 0)
    colh = lax.broadcasted_iota(jnp.int32, (rows, c), 1) // HEAD_DIM
    q = q_ref[0].astype(F32)
    q_rows = jnp.concatenate([jnp.broadcast_to(q[t:t + 1], (N_HEADS, c)) for t in range(nq)], axis=0)
    head_sel = (row % N_HEADS) == colh
    qbd = jnp.where(head_sel, q_rows, 0.0)
    bias = bias_ref[...]
    m2 = m2_ref[...]

    def block(kb, vb, mask):
        z = jnp.dot(qbd, kb, preferred_element_type=F32) + bias
        lom = _log_one_minus_beta(z)
        if mask is not None:
            lom = jnp.where(mask, lom, 0.0)
        later, total = _suffix_sums(lom, m2)
        w = jnp.exp(lom + z + later + run_ref[...])
        if mask is not None:
            w = jnp.where(mask, w, 0.0)
        acc_ref[...] += lax.dot_general(w, vb, (((1,), (1,)), ((), ())), preferred_element_type=F32)
        run_ref[...] += total

    @pl.when(s == 0)
    def _():
        acc_ref[...] = jnp.zeros_like(acc_ref)
        run_ref[...] = jnp.zeros_like(run_ref)
        key = lax.broadcasted_iota(jnp.int32, (rows, KEY_BLOCK), 1)
        qi = lax.broadcasted_iota(jnp.int32, (rows, KEY_BLOCK), 0) // N_HEADS
        block(knew_ref[0], vnew_ref[0], key < qi)

    for p in range(pages_per_step):
        block(k_refs[p][0], v_refs[p][0], None)

    @pl.when(s == pl.num_programs(1) - 1)
    def _():
        picked = jnp.where(head_sel, acc_ref[...], 0.0)
        out = [jnp.sum(picked[t * N_HEADS:(t + 1) * N_HEADS], axis=0, keepdims=True) for t in range(nq)]
        o_ref[0] = jnp.concatenate(out, axis=0)


def _pages_transposed(cache):
    _, n_pool, page, h, hd = cache.shape
    return jnp.transpose(cache, (0, 1, 3, 4, 2)).reshape(n_pool, h * hd, page)


def _sample_attn(q, k_new, v_new, cache_k, cache_v, page_table, sb_bias, m2, pages_per_step=8):
    b, nq, c = q.shape
    n_pages = page_table.shape[1]
    rows = nq * N_HEADS
    bias = jnp.broadcast_to(jnp.tile(sb_bias, nq)[:, None], (rows, KEY_BLOCK)).astype(F32)
    pad = ((0, 0), (0, 0), (0, KEY_BLOCK - nq))
    k_new = jnp.pad(jnp.swapaxes(k_new, 1, 2), pad)
    v_new = jnp.pad(jnp.swapaxes(v_new, 1, 2), pad)

    def page_spec(p):
        return pl.BlockSpec((1, c, KEY_BLOCK),
                            lambda bb, s, pt: (pt[bb, n_pages - 1 - (s * pages_per_step + p)], 0, 0))

    per_seq = lambda shape: pl.BlockSpec(shape, lambda bb, s, pt: (bb, 0, 0))
    grid_spec = pltpu.PrefetchScalarGridSpec(
        num_scalar_prefetch=1,
        grid=(b, n_pages // pages_per_step),
        in_specs=[per_seq((1, nq, c)),
                  pl.BlockSpec((rows, KEY_BLOCK), lambda bb, s, pt: (0, 0)),
                  per_seq((1, c, KEY_BLOCK)), per_seq((1, c, KEY_BLOCK)),
                  pl.BlockSpec(m2.shape, lambda bb, s, pt: (0, 0))]
                 + [page_spec(p) for p in range(pages_per_step)] * 2,
        out_specs=per_seq((1, nq, c)),
        scratch_shapes=[pltpu.VMEM((rows, c), F32), pltpu.VMEM((rows, KEY_BLOCK), F32)],
    )
    return pl.pallas_call(
        functools.partial(_sample_attn_kernel, pages_per_step),
        grid_spec=grid_spec,
        out_shape=jax.ShapeDtypeStruct((b, nq, c), F32),
        compiler_params=pltpu.CompilerParams(
            dimension_semantics=("parallel", "arbitrary"), vmem_limit_bytes=VMEM_LIMIT),
        name="sample_attn",
    )(page_table, q, bias, k_new, v_new, m2,
      *([cache_k] * pages_per_step), *([cache_v] * pages_per_step))


def _conv_kernel(zero_first_halo, u_ref, halo_ref, w_ref, b_ref, g_ref, beta_ref, s_ref, win_ref):
    tm = u_ref.shape[1]
    if zero_first_halo:
        first = pl.program_id(1) == 0

        @pl.when(first)
        def _():
            win_ref[0:HALO, :] = jnp.zeros((HALO, win_ref.shape[1]), F32)

        @pl.when(jnp.logical_not(first))
        def _():
            win_ref[0:HALO, :] = halo_ref[0]
    else:
        win_ref[0:HALO, :] = halo_ref[0]
    win_ref[HALO:HALO + tm, :] = u_ref[0]
    rc = min(tm, 32)
    lead = HALO - (CONV_W - 1)
    for r0 in range(0, tm, rc):
        acc = jnp.broadcast_to(b_ref[...], (rc, b_ref.shape[1]))
        for tap in range(CONV_W):
            acc = acc + w_ref[tap:tap + 1, :] * win_ref[r0 + lead + tap:r0 + lead + tap + rc, :]
        y = _ln(acc) * g_ref[...] + beta_ref[...]
        s_ref[0, r0:r0 + rc, :] = _silu(y).astype(s_ref.dtype)


def _conv_module(u, halo, zero_first_halo, conv_w, conv_b, cln_g, cln_b, tm):
    b, t, c = u.shape
    per_tile = tm // HALO
    if zero_first_halo:
        halo_spec = pl.BlockSpec((1, HALO, c), lambda bb, i: (bb, jnp.maximum(i * per_tile - 1, 0), 0))
    else:
        halo_spec = pl.BlockSpec((1, HALO, c), lambda bb, i: (bb, 0, 0))
    vec = pl.BlockSpec((1, c), lambda bb, i: (0, 0))
    return pl.pallas_call(
        functools.partial(_conv_kernel, zero_first_halo),
        grid=(b, t // tm),
        in_specs=[pl.BlockSpec((1, tm, c), lambda bb, i: (bb, i, 0)), halo_spec,
                  pl.BlockSpec((CONV_W, c), lambda bb, i: (0, 0)), vec, vec, vec],
        out_specs=pl.BlockSpec((1, tm, c), lambda bb, i: (bb, i, 0)),
        out_shape=jax.ShapeDtypeStruct((b, t, c), BF16 if tm % 16 == 0 else F32),
        scratch_shapes=[pltpu.VMEM((HALO + tm, c), F32)],
        compiler_params=pltpu.CompilerParams(dimension_semantics=("parallel", "parallel"),
                                             vmem_limit_bytes=VMEM_LIMIT),
        name="conv_module",
    )(u, halo, conv_w, conv_b.reshape(1, c), cln_g.reshape(1, c), cln_b.reshape(1, c))


def _post_kernel(x_ref, attn_ref, s_ref, gate1_ref, shift2_ref, scale2_ref, gate2_ref,
                 wo_ref, g1_ref, b1_ref, wgu_ref, wdown_ref, g2_ref, b2_ref, y_ref, act_ref):
    c = attn_ref.shape[1]
    d_ff = wdown_ref.shape[0]
    mix = jnp.dot(attn_ref[...].astype(BF16), wo_ref[0:c, :], preferred_element_type=F32)
    mix = mix + jnp.dot(s_ref[...].astype(BF16), wo_ref[c:, :], preferred_element_type=F32)
    x1 = _ln(ALPHA * x_ref[...] + (1.0 + gate1_ref[...]) * mix) * g1_ref[...] + b1_ref[...]
    h2 = (_ln(x1) * (1.0 + scale2_ref[...]) + shift2_ref[...]).astype(BF16)
    n_chunks = 2
    fc = d_ff // n_chunks
    for j in range(n_chunks):
        gt = jnp.dot(h2, wgu_ref[:, j * fc:(j + 1) * fc], preferred_element_type=F32)
        up = jnp.dot(h2, wgu_ref[:, d_ff + j * fc:d_ff + (j + 1) * fc], preferred_element_type=F32)
        act_ref[:, j * fc:(j + 1) * fc] = (_silu(gt) * up).astype(BF16)
    f = jnp.dot(act_ref[...], wdown_ref[...], preferred_element_type=F32)
    y_ref[...] = _ln(ALPHA * x1 + (1.0 + gate2_ref[...]) * f) * g2_ref[...] + b2_ref[...]


def _post(x, attn, s, mod, rows_per_group, w_o, ln1_g, ln1_b, w_gu, w_down, ln2_g, ln2_b, tm):
    n, d = x.shape
    c = attn.shape[1]
    d_ff = w_down.shape[0]
    tok = lambda w: pl.BlockSpec((tm, w), lambda i: (i, 0))
    vec = lambda a: a.reshape(1, d)
    return pl.pallas_call(
        _post_kernel,
        grid=(n // tm,),
        in_specs=[tok(d), tok(c), tok(s.shape[1])]
                 + [_mod_spec(mod, tm, rows_per_group, ch) for ch in (2, 3, 4, 5)]
                 + [_const_spec((d, d)), _const_spec((1, d)), _const_spec((1, d)),
                    _const_spec((d, 2 * d_ff)), _const_spec((d_ff, d)), _const_spec((1, d)), _const_spec((1, d))],
        out_specs=tok(d),
        out_shape=jax.ShapeDtypeStruct((n, d), F32),
        scratch_shapes=[pltpu.VMEM((tm, d_ff), BF16)],
        compiler_params=pltpu.CompilerParams(dimension_semantics=("parallel",),
                                             vmem_limit_bytes=VMEM_LIMIT),
        name="post",
    )(x, attn, s, mod, mod, mod, mod, w_o, vec(ln1_g), vec(ln1_b), w_gu, w_down, vec(ln2_g), vec(ln2_b))


def kernel(x_prompt, x_sample, c_prompt, c_sample, cache_k, cache_v, page_table, state_conv, w_ada, b_ada, w_in, sb_bias, conv_w, conv_b, cln_g, cln_b, w_o, ln1_g, ln1_b, w_gu, w_down, ln2_g, ln2_b):
    assert w_in.shape[0] == 1, "single-layer model"
    bp, tp, d = x_prompt.shape
    bs, ts, _ = x_sample.shape
    n_pool, page, _, _ = cache_k.shape[1:]
    c = C_ATTN
    cc = state_conv.shape[-1]
    hist = CONV_W - 1

    w_in_b = w_in[0].astype(BF16)
    w_o_b = w_o[0].astype(BF16)
    w_gu_b = w_gu[0].astype(BF16)
    w_down_b = w_down[0].astype(BF16)
    m2 = _suffix_matrix()

    mod = _ada(jnp.concatenate([c_prompt, c_sample], axis=0), w_ada[0], b_ada[0])
    mod_p = mod[:bp].reshape(bp, 1, 6 * d)
    mod_s = jnp.repeat(mod[bp:], ts, axis=0).reshape(1, bs * ts, 6 * d)

    xp = x_prompt.reshape(bp * tp, d)
    q_p, k_p, v_p, kb_p, vb_p, u_p = _inproj(xp, mod_p, tp, w_in_b, tm=512)
    attn_p = _prompt_attn(q_p.reshape(bp, tp, c), kb_p.reshape(bp, tp, c), vb_p.reshape(bp, tp, c),
                          sb_bias[0], m2)
    u_p3 = u_p.reshape(bp, tp, cc)
    s_p = _conv_module(u_p3, u_p3, True, conv_w[0], conv_b[0], cln_g[0], cln_b[0], tm=128)
    y_p = _post(xp, attn_p.reshape(bp * tp, c), s_p.reshape(bp * tp, cc), mod_p, tp,
                w_o_b, ln1_g[0], ln1_b[0], w_gu_b, w_down_b, ln2_g[0], ln2_b[0], tm=256)

    xs = x_sample.reshape(bs * ts, d)
    q_s, k_s, v_s, _, _, u_s = _inproj(xs, mod_s, ts, w_in_b, tm=bs * ts)
    attn_s = _sample_attn(q_s.reshape(bs, ts, c), k_s.reshape(bs, ts, c), v_s.reshape(bs, ts, c),
                          _pages_transposed(cache_k), _pages_transposed(cache_v),
                          page_table, sb_bias[0], m2)
    u_s3 = u_s.reshape(bs, ts, cc)
    state_pad = jnp.pad(state_conv[0], ((0, 0), (HALO - hist, 0), (0, 0)))
    s_s = _conv_module(u_s3, state_pad, False, conv_w[0], conv_b[0], cln_g[0], cln_b[0], tm=ts)
    y_s = _post(xs, attn_s.reshape(bs * ts, c), s_s.reshape(bs * ts, cc), mod_s, ts,
                w_o_b, ln1_g[0], ln1_b[0], w_gu_b, w_down_b, ln2_g[0], ln2_b[0], tm=bs * ts)

    heads = lambda a, b, t: a.reshape(1, b, t, N_HEADS, HEAD_DIM)
    new_conv_p = u_p3[:, tp - hist:][None]
    new_conv_s = jnp.concatenate([state_conv[0][:, ts:], u_s3], axis=1)[None]
    return (y_p.reshape(bp, tp, d), y_s.reshape(bs, ts, d),
            heads(k_p, bp, tp), heads(v_p, bp, tp), new_conv_p,
            heads(k_s, bs, ts), heads(v_s, bs, ts), new_conv_s)
```

```python
import functools

import jax
import jax.numpy as jnp
from jax import lax
from jax.experimental import pallas as pl
from jax.experimental.pallas import tpu as pltpu

F32 = jnp.float32
BF16 = jnp.bfloat16

HEAD_DIM = 64
N_HEADS = 8
C_ATTN = N_HEADS * HEAD_DIM
CONV_W = 31
SUBLANES = 8
HALO = 32
LN_EPS = 1e-5
ALPHA = 2.0 ** 0.25
KEY_BLOCK = 128
LOG2E = 1.4426950408889634
Q_SCALE = HEAD_DIM ** -0.5 * LOG2E
Q_TILE = 512
LOOP_BLOCKS = 2
VMEM_LIMIT = 56 * 1024 * 1024


def _ln(x):
    mu = jnp.mean(x, axis=-1, keepdims=True)
    xc = x - mu
    var = jnp.mean(xc * xc, axis=-1, keepdims=True)
    return xc * lax.rsqrt(var + LN_EPS)


def _silu(x):
    return x * jax.nn.sigmoid(x)


def _stick_logs(z2):
    ls = jnp.minimum(z2, 0.0) - jnp.log2(1.0 + jnp.exp2(-jnp.abs(z2)))
    return ls, ls - z2


def _suffix_matrix():
    r = lax.broadcasted_iota(jnp.int32, (2 * KEY_BLOCK, 2 * KEY_BLOCK), 0) % KEY_BLOCK
    c = lax.broadcasted_iota(jnp.int32, (2 * KEY_BLOCK, 2 * KEY_BLOCK), 1)
    return jnp.where((c >= KEY_BLOCK) | (r > c), 1.0, 0.0).astype(BF16)


def _split_bf16(x):
    hi = x.astype(BF16)
    return hi, (x - hi.astype(F32)).astype(BF16)


def _ada_kernel(c_ref, w_ref, b_ref, o_ref):
    h = _silu(c_ref[...])
    o_ref[...] = jnp.dot(h, w_ref[...], preferred_element_type=F32) + b_ref[...]


def _ada(c, w_ada, b_ada):
    n, d = c.shape
    dn = w_ada.shape[1]
    tn = 1024
    return pl.pallas_call(
        _ada_kernel,
        grid=(dn // tn,),
        in_specs=[pl.BlockSpec((n, d), lambda j: (0, 0)),
                  pl.BlockSpec((d, tn), lambda j: (0, j)),
                  pl.BlockSpec((1, tn), lambda j: (0, j))],
        out_specs=pl.BlockSpec((n, tn), lambda j: (0, j)),
        out_shape=jax.ShapeDtypeStruct((n, dn), F32),
        compiler_params=pltpu.CompilerParams(dimension_semantics=("arbitrary",),
                                             vmem_limit_bytes=VMEM_LIMIT),
        name="ada",
    )(c, w_ada, b_ada.reshape(1, dn))


def _inproj_kernel(kv_transposed, x_ref, shift_ref, scale_ref, w_ref, q_ref, k_ref, v_ref, kb_ref, vb_ref, u_ref):
    c = C_ATTN
    h = (_ln(x_ref[...]) * (1.0 + scale_ref[...]) + shift_ref[...]).astype(BF16)
    q = jnp.dot(h, w_ref[:, 0:c], preferred_element_type=F32)
    q_ref[...] = (q * Q_SCALE).astype(BF16)
    k = jnp.dot(h, w_ref[:, c:2 * c], preferred_element_type=F32)
    kb_ref[...] = k.astype(BF16)
    v = jnp.dot(h, w_ref[:, 2 * c:3 * c], preferred_element_type=F32)
    vb_ref[...] = v.astype(BF16)
    if kv_transposed:
        k_ref[0] = k.T
        v_ref[0] = v.T
    else:
        k_ref[...] = k
        v_ref[...] = v
    cc = (w_ref.shape[1] - 3 * c) // 2
    a = jnp.dot(h, w_ref[:, 3 * c:3 * c + cc], preferred_element_type=F32)
    g = jnp.dot(h, w_ref[:, 3 * c + cc:], preferred_element_type=F32)
    u_ref[...] = a * jax.nn.sigmoid(g)


def _mod_spec(mod, tm, rows_per_group, chunk):
    d = mod.shape[-1] // 6
    if mod.shape[1] == 1:
        return pl.BlockSpec((None, 1, d), lambda i: ((i * tm) // rows_per_group, 0, chunk))
    return pl.BlockSpec((None, tm, d), lambda i: (0, i, chunk))


def _const_spec(shape):
    return pl.BlockSpec(shape, lambda *_: (0,) * len(shape), pipeline_mode=pl.Buffered(1))


def _inproj(x, mod, rows_per_group, w_in, tm, kv_transposed):
    n, d = x.shape
    d_in = w_in.shape[1]
    c = C_ATTN
    cc = (d_in - 3 * c) // 2
    tok = lambda w: pl.BlockSpec((tm, w), lambda i: (i, 0))
    sds = jax.ShapeDtypeStruct
    if kv_transposed:
        tiles = rows_per_group // tm
        kv_spec = pl.BlockSpec((1, c, tm), lambda i: (i // tiles, 0, i % tiles))
        kv_shape = sds((n // rows_per_group, c, rows_per_group), F32)
    else:
        kv_spec, kv_shape = tok(c), sds((n, c), F32)
    return pl.pallas_call(
        functools.partial(_inproj_kernel, kv_transposed),
        grid=(n // tm,),
        in_specs=[tok(d), _mod_spec(mod, tm, rows_per_group, 0), _mod_spec(mod, tm, rows_per_group, 1),
                  _const_spec((d, d_in))],
        out_specs=[tok(c), kv_spec, kv_spec, tok(c), tok(c), tok(cc)],
        out_shape=[sds((n, c), BF16), kv_shape, kv_shape, sds((n, c), BF16), sds((n, c), BF16), sds((n, cc), F32)],
        compiler_params=pltpu.CompilerParams(dimension_semantics=("parallel",),
                                             vmem_limit_bytes=VMEM_LIMIT),
        name="inproj",
    )(x, mod, mod, w_in)


def _prompt_attn_kernel(bias_ref, q_ref, k_ref, v_ref, m2_ref, o_ref, acc_ref, run_ref):
    hp = pl.program_id(1)
    i = pl.program_id(2)
    tq = q_ref.shape[1]
    kb_per_tile = tq // KEY_BLOCK
    hw = q_ref.shape[2]
    m2 = m2_ref[...]
    lane = lax.broadcasted_iota(jnp.int32, (KEY_BLOCK, hw), 1)
    head_a = lane < HEAD_DIM
    col2 = lax.broadcasted_iota(jnp.int32, (1, 2 * KEY_BLOCK), 1)
    bias2 = jnp.where(col2 < KEY_BLOCK, bias_ref[2 * hp], bias_ref[2 * hp + 1])

    def block(j, r0, masked):
        start = pl.multiple_of(j * KEY_BLOCK, KEY_BLOCK)
        kb = k_ref[0, pl.ds(start, KEY_BLOCK), :]
        vb = v_ref[0, pl.ds(start, KEY_BLOCK), :]
        zero = jnp.zeros_like(kb)
        k2 = jnp.concatenate([jnp.where(head_a, kb, zero), jnp.where(head_a, zero, kb)], axis=0)
        v2 = jnp.concatenate([jnp.where(head_a, vb, zero), jnp.where(head_a, zero, vb)], axis=0)
        z = lax.dot_general(q_ref[0, r0:, :], k2, (((1,), (1,)), ((), ())),
                            preferred_element_type=F32) + bias2
        ls, lom = _stick_logs(z)
        if masked:
            n = tq - r0
            key = lax.broadcasted_iota(jnp.int32, (n, 2 * KEY_BLOCK), 1) % KEY_BLOCK
            qrow = lax.broadcasted_iota(jnp.int32, (n, 2 * KEY_BLOCK), 0)
            mask = key < qrow
            lom = jnp.where(mask, lom, 0.0)
        hi, lo = _split_bf16(lom)
        cs = [jnp.dot(jnp.concatenate([hi[:, h * KEY_BLOCK:(h + 1) * KEY_BLOCK],
                                       lo[:, h * KEY_BLOCK:(h + 1) * KEY_BLOCK]], axis=1),
                      m2, preferred_element_type=F32) for h in range(2)]
        later = jnp.concatenate([cs[0][:, :KEY_BLOCK], cs[1][:, :KEY_BLOCK]], axis=1)
        total = jnp.concatenate([cs[0][:, KEY_BLOCK:], cs[1][:, KEY_BLOCK:]], axis=1)
        w = jnp.exp2(ls + later + run_ref[r0:, :])
        if masked:
            w = jnp.where(mask, w, 0.0)
        acc_ref[r0:, :] += jnp.dot(w.astype(BF16), v2, preferred_element_type=F32)
        run_ref[r0:, :] += total

    acc_ref[...] = jnp.zeros_like(acc_ref)
    run_ref[...] = jnp.zeros_like(run_ref)
    for jj in reversed(range(kb_per_tile)):
        block(i * kb_per_tile + jj, jj * KEY_BLOCK, True)

    def body(t, carry):
        for u in range(LOOP_BLOCKS):
            block(i * kb_per_tile - 1 - (t * LOOP_BLOCKS + u), 0, False)
        return carry

    lax.fori_loop(0, i * (kb_per_tile // LOOP_BLOCKS), body, 0)
    o_ref[0] = acc_ref[...].astype(o_ref.dtype)


def _prompt_attn(q, k, v, sb_bias, m2):
    b, t, c = q.shape
    tq = Q_TILE
    hw = 2 * HEAD_DIM
    assert t % tq == 0 and (tq // KEY_BLOCK) % LOOP_BLOCKS == 0
    grid_spec = pltpu.PrefetchScalarGridSpec(
        num_scalar_prefetch=1,
        grid=(b, c // hw, t // tq),
        in_specs=[pl.BlockSpec((1, tq, hw), lambda bb, hp, i, bias: (bb, i, hp)),
                  pl.BlockSpec((1, t, hw), lambda bb, hp, i, bias: (bb, 0, hp)),
                  pl.BlockSpec((1, t, hw), lambda bb, hp, i, bias: (bb, 0, hp)),
                  pl.BlockSpec(m2.shape, lambda bb, hp, i, bias: (0, 0))],
        out_specs=pl.BlockSpec((1, tq, hw), lambda bb, hp, i, bias: (bb, i, hp)),
        scratch_shapes=[pltpu.VMEM((tq, hw), F32), pltpu.VMEM((tq, 2 * KEY_BLOCK), F32)],
    )
    return pl.pallas_call(
        _prompt_attn_kernel,
        grid_spec=grid_spec,
        out_shape=jax.ShapeDtypeStruct((b, t, c), BF16),
        compiler_params=pltpu.CompilerParams(
            dimension_semantics=("parallel", "parallel", "arbitrary"), vmem_limit_bytes=VMEM_LIMIT),
        name="prompt_attn",
    )(sb_bias, q, k, v, m2)


def _sample_attn_kernel(pages_per_step, pt_ref, q_ref, bias_ref, knew_ref, vnew_ref, m2_ref, *refs):
    k_refs = refs[:pages_per_step]
    v_refs = refs[pages_per_step:2 * pages_per_step]
    o_ref, acc_ref, run_ref = refs[2 * pages_per_step:]
    s = pl.program_id(1)
    nq = q_ref.shape[1]
    rows = nq * N_HEADS
    c = q_ref.shape[2]
    row = lax.broadcasted_iota(jnp.int32, (rows, c), 0)
    colh = lax.broadcasted_iota(jnp.int32, (rows, c), 1) // HEAD_DIM
    q = q_ref[0].astype(F32)
    q_rows = jnp.concatenate([jnp.broadcast_to(q[t:t + 1], (N_HEADS, c)) for t in range(nq)], axis=0)
    head_sel = (row % N_HEADS) == colh
    qbd = jnp.where(head_sel, q_rows, 0.0)
    bias = bias_ref[...]
    m2 = m2_ref[...]

    def attend(kt, vt, mask):
        n = kt.shape[1] // KEY_BLOCK
        blk = lambda a, p: a[:, p * KEY_BLOCK:(p + 1) * KEY_BLOCK]
        z = jnp.dot(qbd, kt, preferred_element_type=F32) + jnp.concatenate([bias] * n, axis=1)
        ls, lom = _stick_logs(z)
        if mask is not None:
            lom = jnp.where(mask, lom, 0.0)
        hi, lo = _split_bf16(lom)
        lhs = jnp.concatenate([jnp.concatenate([blk(hi, p), blk(lo, p)], axis=1) for p in range(n)], axis=0)
        cs = jnp.dot(lhs, m2, preferred_element_type=F32)
        run = run_ref[...]
        args = []
        for p in range(n):
            csp = cs[p * rows:(p + 1) * rows]
            args.append(blk(ls, p) + csp[:, :KEY_BLOCK] + run)
            run = run + csp[:, KEY_BLOCK:]
        w = jnp.exp2(jnp.concatenate(args, axis=1))
        if mask is not None:
            w = jnp.where(mask, w, 0.0)
        acc_ref[...] += lax.dot_general(w, vt, (((1,), (1,)), ((), ())), preferred_element_type=F32)
        run_ref[...] = run

    @pl.when(s == 0)
    def _():
        acc_ref[...] = jnp.zeros_like(acc_ref)
        run_ref[...] = jnp.zeros_like(run_ref)
        key = lax.broadcasted_iota(jnp.int32, (rows, KEY_BLOCK), 1)
        qi = lax.broadcasted_iota(jnp.int32, (rows, KEY_BLOCK), 0) // N_HEADS
        attend(knew_ref[0], vnew_ref[0], key < qi)

    attend(jnp.concatenate([r[0] for r in k_refs], axis=1),
           jnp.concatenate([r[0] for r in v_refs], axis=1), None)

    @pl.when(s == pl.num_programs(1) - 1)
    def _():
        picked = jnp.where(head_sel, acc_ref[...], 0.0)
        out = [jnp.sum(picked[t * N_HEADS:(t + 1) * N_HEADS], axis=0, keepdims=True) for t in range(nq)]
        o_ref[0] = jnp.concatenate(out, axis=0)


def _pages_transposed(cache):
    _, n_pool, page, h, hd = cache.shape
    return jnp.transpose(cache, (0, 1, 3, 4, 2)).reshape(n_pool, h * hd, page)


def _sample_attn(q, k_new, v_new, cache_k, cache_v, page_table, sb_bias, m2, pages_per_step=16):
    b, nq, c = q.shape
    n_pages = page_table.shape[1]
    rows = nq * N_HEADS
    bias = jnp.broadcast_to(jnp.tile(sb_bias, nq)[:, None], (rows, KEY_BLOCK)).astype(F32)
    pad = ((0, 0), (0, 0), (0, KEY_BLOCK - nq))
    k_new = jnp.pad(jnp.swapaxes(k_new, 1, 2), pad)
    v_new = jnp.pad(jnp.swapaxes(v_new, 1, 2), pad)

    def page_spec(p):
        return pl.BlockSpec((1, c, KEY_BLOCK),
                            lambda bb, s, pt: (pt[bb, n_pages - 1 - (s * pages_per_step + p)], 0, 0))

    per_seq = lambda shape: pl.BlockSpec(shape, lambda bb, s, pt: (bb, 0, 0))
    grid_spec = pltpu.PrefetchScalarGridSpec(
        num_scalar_prefetch=1,
        grid=(b, n_pages // pages_per_step),
        in_specs=[per_seq((1, nq, c)),
                  pl.BlockSpec((rows, KEY_BLOCK), lambda bb, s, pt: (0, 0)),
                  per_seq((1, c, KEY_BLOCK)), per_seq((1, c, KEY_BLOCK)),
                  pl.BlockSpec(m2.shape, lambda bb, s, pt: (0, 0))]
                 + [page_spec(p) for p in range(pages_per_step)] * 2,
        out_specs=per_seq((1, nq, c)),
        scratch_shapes=[pltpu.VMEM((rows, c), F32), pltpu.VMEM((rows, KEY_BLOCK), F32)],
    )
    return pl.pallas_call(
        functools.partial(_sample_attn_kernel, pages_per_step),
        grid_spec=grid_spec,
        out_shape=jax.ShapeDtypeStruct((b, nq, c), F32),
        compiler_params=pltpu.CompilerParams(
            dimension_semantics=("parallel", "arbitrary"), vmem_limit_bytes=VMEM_LIMIT),
        name="sample_attn",
    )(page_table, q, bias, k_new, v_new, m2,
      *([cache_k] * pages_per_step), *([cache_v] * pages_per_step))


def _conv_kernel(zero_first_halo, u_ref, halo_ref, w_ref, b_ref, g_ref, beta_ref, s_ref, win_ref, shift_ref):
    tm = u_ref.shape[1]
    if zero_first_halo:
        first = pl.program_id(1) == 0

        @pl.when(first)
        def _():
            win_ref[0:HALO, :] = jnp.zeros((HALO, win_ref.shape[1]), F32)

        @pl.when(jnp.logical_not(first))
        def _():
            win_ref[0:HALO, :] = halo_ref[0]
    else:
        win_ref[0:HALO, :] = halo_ref[0]
    win_ref[HALO:HALO + tm, :] = u_ref[0]
    rc = min(tm, 32)
    lead = HALO - (CONV_W - 1)
    if tm % SUBLANES == 0:
        n = HALO + tm - SUBLANES
        for r in range(1, SUBLANES):
            shift_ref[r - 1, 0:n, :] = win_ref[r:r + n, :]

        def tap_rows(start):
            r = start % SUBLANES
            if r == 0:
                return win_ref[start:start + rc, :]
            return shift_ref[r - 1, start - r:start - r + rc, :]
    else:
        def tap_rows(start):
            return win_ref[start:start + rc, :]

    for r0 in range(0, tm, rc):
        acc = jnp.broadcast_to(b_ref[...], (rc, b_ref.shape[1]))
        for tap in range(CONV_W):
            acc = acc + w_ref[tap:tap + 1, :] * tap_rows(r0 + lead + tap)
        y = _ln(acc) * g_ref[...] + beta_ref[...]
        s_ref[0, r0:r0 + rc, :] = _silu(y).astype(s_ref.dtype)


def _conv_module(u, halo, zero_first_halo, conv_w, conv_b, cln_g, cln_b, tm):
    b, t, c = u.shape
    per_tile = tm // HALO
    if zero_first_halo:
        halo_spec = pl.BlockSpec((1, HALO, c), lambda bb, i: (bb, jnp.maximum(i * per_tile - 1, 0), 0))
    else:
        halo_spec = pl.BlockSpec((1, HALO, c), lambda bb, i: (bb, 0, 0))
    vec = pl.BlockSpec((1, c), lambda bb, i: (0, 0))
    return pl.pallas_call(
        functools.partial(_conv_kernel, zero_first_halo),
        grid=(b, t // tm),
        in_specs=[pl.BlockSpec((1, tm, c), lambda bb, i: (bb, i, 0)), halo_spec,
                  pl.BlockSpec((CONV_W, c), lambda bb, i: (0, 0)), vec, vec, vec],
        out_specs=pl.BlockSpec((1, tm, c), lambda bb, i: (bb, i, 0)),
        out_shape=jax.ShapeDtypeStruct((b, t, c), BF16 if tm % 16 == 0 else F32),
        scratch_shapes=[pltpu.VMEM((HALO + tm, c), F32), pltpu.VMEM((SUBLANES - 1, HALO + tm, c), F32)],
        compiler_params=pltpu.CompilerParams(dimension_semantics=("parallel", "parallel"),
                                             vmem_limit_bytes=VMEM_LIMIT),
        name="conv_module",
    )(u, halo, conv_w, conv_b.reshape(1, c), cln_g.reshape(1, c), cln_b.reshape(1, c))


def _post_kernel(x_ref, attn_ref, s_ref, gate1_ref, shift2_ref, scale2_ref, gate2_ref,
                 wo_ref, g1_ref, b1_ref, wgu_ref, wdown_ref, g2_ref, b2_ref, y_ref, act_ref):
    c = attn_ref.shape[1]
    d_ff = wdown_ref.shape[0]
    mix = jnp.dot(attn_ref[...].astype(BF16), wo_ref[0:c, :], preferred_element_type=F32)
    mix = mix + jnp.dot(s_ref[...].astype(BF16), wo_ref[c:, :], preferred_element_type=F32)
    x1 = _ln(ALPHA * x_ref[...] + (1.0 + gate1_ref[...]) * mix) * g1_ref[...] + b1_ref[...]
    h2 = (_ln(x1) * (1.0 + scale2_ref[...]) + shift2_ref[...]).astype(BF16)
    n_chunks = 2
    fc = d_ff // n_chunks
    for j in range(n_chunks):
        gt = jnp.dot(h2, wgu_ref[:, j * fc:(j + 1) * fc], preferred_element_type=F32)
        up = jnp.dot(h2, wgu_ref[:, d_ff + j * fc:d_ff + (j + 1) * fc], preferred_element_type=F32)
        act_ref[:, j * fc:(j + 1) * fc] = (_silu(gt) * up).astype(BF16)
    f = jnp.dot(act_ref[...], wdown_ref[...], preferred_element_type=F32)
    y_ref[...] = _ln(ALPHA * x1 + (1.0 + gate2_ref[...]) * f) * g2_ref[...] + b2_ref[...]


def _post(x, attn, s, mod, rows_per_group, w_o, ln1_g, ln1_b, w_gu, w_down, ln2_g, ln2_b, tm):
    n, d = x.shape
    c = attn.shape[1]
    d_ff = w_down.shape[0]
    tok = lambda w: pl.BlockSpec((tm, w), lambda i: (i, 0))
    vec = lambda a: a.reshape(1, d)
    return pl.pallas_call(
        _post_kernel,
        grid=(n // tm,),
        in_specs=[tok(d), tok(c), tok(s.shape[1])]
                 + [_mod_spec(mod, tm, rows_per_group, ch) for ch in (2, 3, 4, 5)]
                 + [_const_spec((d, d)), _const_spec((1, d)), _const_spec((1, d)),
                    _const_spec((d, 2 * d_ff)), _const_spec((d_ff, d)), _const_spec((1, d)), _const_spec((1, d))],
        out_specs=tok(d),
        out_shape=jax.ShapeDtypeStruct((n, d), F32),
        scratch_shapes=[pltpu.VMEM((tm, d_ff), BF16)],
        compiler_params=pltpu.CompilerParams(dimension_semantics=("parallel",),
                                             vmem_limit_bytes=VMEM_LIMIT),
        name="post",
    )(x, attn, s, mod, mod, mod, mod, w_o, vec(ln1_g), vec(ln1_b), w_gu, w_down, vec(ln2_g), vec(ln2_b))


def kernel(x_prompt, x_sample, c_prompt, c_sample, cache_k, cache_v, page_table, state_conv, w_ada, b_ada, w_in, sb_bias, conv_w, conv_b, cln_g, cln_b, w_o, ln1_g, ln1_b, w_gu, w_down, ln2_g, ln2_b):
    assert w_in.shape[0] == 1, "single-layer model"
    bp, tp, d = x_prompt.shape
    bs, ts, _ = x_sample.shape
    n_pool, page, _, _ = cache_k.shape[1:]
    c = C_ATTN
    cc = state_conv.shape[-1]
    hist = CONV_W - 1

    w_in_b = w_in[0].astype(BF16)
    w_o_b = w_o[0].astype(BF16)
    w_gu_b = w_gu[0].astype(BF16)
    w_down_b = w_down[0].astype(BF16)
    m2 = _suffix_matrix()
    bias2 = sb_bias[0] * LOG2E

    mod = _ada(jnp.concatenate([c_prompt, c_sample], axis=0), w_ada[0], b_ada[0])
    mod_p = mod[:bp].reshape(bp, 1, 6 * d)
    mod_s = jnp.repeat(mod[bp:], ts, axis=0).reshape(1, bs * ts, 6 * d)

    xp = x_prompt.reshape(bp * tp, d)
    q_p, kt_p, vt_p, kb_p, vb_p, u_p = _inproj(xp, mod_p, tp, w_in_b, tm=512, kv_transposed=True)
    attn_p = _prompt_attn(q_p.reshape(bp, tp, c), kb_p.reshape(bp, tp, c), vb_p.reshape(bp, tp, c),
                          bias2, m2)
    u_p3 = u_p.reshape(bp, tp, cc)
    s_p = _conv_module(u_p3, u_p3, True, conv_w[0], conv_b[0], cln_g[0], cln_b[0], tm=256)
    y_p = _post(xp, attn_p.reshape(bp * tp, c), s_p.reshape(bp * tp, cc), mod_p, tp,
                w_o_b, ln1_g[0], ln1_b[0], w_gu_b, w_down_b, ln2_g[0], ln2_b[0], tm=256)

    xs = x_sample.reshape(bs * ts, d)
    q_s, k_s, v_s, _, _, u_s = _inproj(xs, mod_s, ts, w_in_b, tm=bs * ts, kv_transposed=False)
    attn_s = _sample_attn(q_s.reshape(bs, ts, c), k_s.reshape(bs, ts, c), v_s.reshape(bs, ts, c),
                          _pages_transposed(cache_k), _pages_transposed(cache_v),
                          page_table, bias2, m2)
    u_s3 = u_s.reshape(bs, ts, cc)
    state_pad = jnp.pad(state_conv[0], ((0, 0), (HALO - hist, 0), (0, 0)))
    s_s = _conv_module(u_s3, state_pad, False, conv_w[0], conv_b[0], cln_g[0], cln_b[0], tm=ts)
    y_s = _post(xs, attn_s.reshape(bs * ts, c), s_s.reshape(bs * ts, cc), mod_s, ts,
                w_o_b, ln1_g[0], ln1_b[0], w_gu_b, w_down_b, ln2_g[0], ln2_b[0], tm=bs * ts)

    heads = lambda a, b, t: a.reshape(1, b, t, N_HEADS, HEAD_DIM)
    heads_t = lambda a: jnp.transpose(a.reshape(1, bp, N_HEADS, HEAD_DIM, tp), (0, 1, 4, 2, 3))
    new_conv_p = u_p3[:, tp - hist:][None]
    new_conv_s = jnp.concatenate([state_conv[0][:, ts:], u_s3], axis=1)[None]
    return (y_p.reshape(bp, tp, d), y_s.reshape(bs, ts, d),
            heads_t(kt_p), heads_t(vt_p), new_conv_p,
            heads(k_s, bs, ts), heads(v_s, bs, ts), new_conv_s)
```

```python
import functools

import jax
import jax.numpy as jnp
from jax import lax
from jax.experimental import pallas as pl
from jax.experimental.pallas import tpu as pltpu

F32 = jnp.float32
BF16 = jnp.bfloat16

HEAD_DIM = 64
N_HEADS = 8
C_ATTN = N_HEADS * HEAD_DIM
CONV_W = 31
SUBLANES = 8
HALO = 32
CONV_ROWS = 32
LN_EPS = 1e-5
ALPHA = 2.0 ** 0.25
KEY_BLOCK = 128
LOG2E = 1.4426950408889634
Q_SCALE = HEAD_DIM ** -0.5 * LOG2E
Q_TILE = 512
LOOP_BLOCKS = 2
POST_PHASES = 4
VMEM_LIMIT = 56 * 1024 * 1024


def _ln(x):
    mu = jnp.mean(x, axis=-1, keepdims=True)
    xc = x - mu
    var = jnp.mean(xc * xc, axis=-1, keepdims=True)
    return xc * lax.rsqrt(var + LN_EPS)


def _silu(x):
    return x * jax.nn.sigmoid(x)


def _stick_logs(z2):
    ls = jnp.minimum(z2, 0.0) - jnp.log2(1.0 + jnp.exp2(-jnp.abs(z2)))
    return ls, ls - z2


def _suffix_matrix():
    r = lax.broadcasted_iota(jnp.int32, (2 * KEY_BLOCK, 2 * KEY_BLOCK), 0) % KEY_BLOCK
    c = lax.broadcasted_iota(jnp.int32, (2 * KEY_BLOCK, 2 * KEY_BLOCK), 1)
    return jnp.where((c >= KEY_BLOCK) | (r > c), 1.0, 0.0).astype(BF16)


def _split_bf16(x):
    hi = x.astype(BF16)
    return hi, (x - hi.astype(F32)).astype(BF16)


def _ada_kernel(c_ref, w_ref, b_ref, o_ref):
    h = _silu(c_ref[...])
    o_ref[...] = jnp.dot(h, w_ref[...], preferred_element_type=F32) + b_ref[...]


def _ada(c, w_ada, b_ada):
    n, d = c.shape
    dn = w_ada.shape[1]
    tn = 1024
    return pl.pallas_call(
        _ada_kernel,
        grid=(dn // tn,),
        in_specs=[pl.BlockSpec((n, d), lambda j: (0, 0)),
                  pl.BlockSpec((d, tn), lambda j: (0, j)),
                  pl.BlockSpec((1, tn), lambda j: (0, j))],
        out_specs=pl.BlockSpec((n, tn), lambda j: (0, j)),
        out_shape=jax.ShapeDtypeStruct((n, dn), F32),
        compiler_params=pltpu.CompilerParams(dimension_semantics=("arbitrary",),
                                             vmem_limit_bytes=VMEM_LIMIT),
        name="ada",
    )(c, w_ada, b_ada.reshape(1, dn))


def _inproj_kernel(kv_transposed, x_ref, shift_ref, scale_ref, w_ref, q_ref, k_ref, v_ref, kb_ref, vb_ref, u_ref):
    c = C_ATTN
    h = (_ln(x_ref[...]) * (1.0 + scale_ref[...]) + shift_ref[...]).astype(BF16)
    q = jnp.dot(h, w_ref[:, 0:c], preferred_element_type=F32)
    q_ref[...] = (q * Q_SCALE).astype(BF16)
    k = jnp.dot(h, w_ref[:, c:2 * c], preferred_element_type=F32)
    kb_ref[...] = k.astype(BF16)
    v = jnp.dot(h, w_ref[:, 2 * c:3 * c], preferred_element_type=F32)
    vb_ref[...] = v.astype(BF16)
    if kv_transposed:
        k_ref[0] = k.T
        v_ref[0] = v.T
    else:
        k_ref[...] = k
        v_ref[...] = v
    cc = (w_ref.shape[1] - 3 * c) // 2
    a = jnp.dot(h, w_ref[:, 3 * c:3 * c + cc], preferred_element_type=F32)
    g = jnp.dot(h, w_ref[:, 3 * c + cc:], preferred_element_type=F32)
    u_ref[...] = a * jax.nn.sigmoid(g)


def _mod_spec(mod, tm, rows_per_group, chunk, tile=lambda i: i):
    d = mod.shape[-1] // 6
    if mod.shape[1] == 1:
        return pl.BlockSpec((None, 1, d), lambda i, *_: ((tile(i) * tm) // rows_per_group, 0, chunk))
    return pl.BlockSpec((None, tm, d), lambda i, *_: (0, tile(i), chunk))


def _const_spec(shape):
    return pl.BlockSpec(shape, lambda *_: (0,) * len(shape), pipeline_mode=pl.Buffered(1))


def _inproj(x, mod, rows_per_group, w_in, tm, kv_transposed):
    n, d = x.shape
    d_in = w_in.shape[1]
    c = C_ATTN
    cc = (d_in - 3 * c) // 2
    tok = lambda w: pl.BlockSpec((tm, w), lambda i: (i, 0))
    sds = jax.ShapeDtypeStruct
    if kv_transposed:
        tiles = rows_per_group // tm
        kv_spec = pl.BlockSpec((1, c, tm), lambda i: (i // tiles, 0, i % tiles))
        kv_shape = sds((n // rows_per_group, c, rows_per_group), F32)
    else:
        kv_spec, kv_shape = tok(c), sds((n, c), F32)
    return pl.pallas_call(
        functools.partial(_inproj_kernel, kv_transposed),
        grid=(n // tm,),
        in_specs=[tok(d), _mod_spec(mod, tm, rows_per_group, 0), _mod_spec(mod, tm, rows_per_group, 1),
                  _const_spec((d, d_in))],
        out_specs=[tok(c), kv_spec, kv_spec, tok(c), tok(c), tok(cc)],
        out_shape=[sds((n, c), BF16), kv_shape, kv_shape, sds((n, c), BF16), sds((n, c), BF16), sds((n, cc), F32)],
        compiler_params=pltpu.CompilerParams(dimension_semantics=("parallel",),
                                             vmem_limit_bytes=VMEM_LIMIT),
        name="inproj",
    )(x, mod, mod, w_in)


def _prompt_attn_kernel(bias_ref, q_ref, k_ref, v_ref, m2_ref, o_ref, acc_ref, run_ref):
    hp = pl.program_id(1)
    i = pl.program_id(2)
    tq = q_ref.shape[1]
    kb_per_tile = tq // KEY_BLOCK
    hw = q_ref.shape[2]
    m2 = m2_ref[...]
    lane = lax.broadcasted_iota(jnp.int32, (KEY_BLOCK, hw), 1)
    head_a = lane < HEAD_DIM
    col2 = lax.broadcasted_iota(jnp.int32, (1, 2 * KEY_BLOCK), 1)
    bias2 = jnp.where(col2 < KEY_BLOCK, bias_ref[2 * hp], bias_ref[2 * hp + 1])

    def block(j, r0, masked):
        start = pl.multiple_of(j * KEY_BLOCK, KEY_BLOCK)
        kb = k_ref[0, pl.ds(start, KEY_BLOCK), :]
        vb = v_ref[0, pl.ds(start, KEY_BLOCK), :]
        zero = jnp.zeros_like(kb)
        k2 = jnp.concatenate([jnp.where(head_a, kb, zero), jnp.where(head_a, zero, kb)], axis=0)
        v2 = jnp.concatenate([jnp.where(head_a, vb, zero), jnp.where(head_a, zero, vb)], axis=0)
        z = lax.dot_general(q_ref[0, r0:, :], k2, (((1,), (1,)), ((), ())),
                            preferred_element_type=F32) + bias2
        ls, lom = _stick_logs(z)
        if masked:
            n = tq - r0
            key = lax.broadcasted_iota(jnp.int32, (n, 2 * KEY_BLOCK), 1) % KEY_BLOCK
            qrow = lax.broadcasted_iota(jnp.int32, (n, 2 * KEY_BLOCK), 0)
            mask = key < qrow
            lom = jnp.where(mask, lom, 0.0)
        hi, lo = _split_bf16(lom)
        cs = [jnp.dot(jnp.concatenate([hi[:, h * KEY_BLOCK:(h + 1) * KEY_BLOCK],
                                       lo[:, h * KEY_BLOCK:(h + 1) * KEY_BLOCK]], axis=1),
                      m2, preferred_element_type=F32) for h in range(2)]
        later = jnp.concatenate([cs[0][:, :KEY_BLOCK], cs[1][:, :KEY_BLOCK]], axis=1)
        total = jnp.concatenate([cs[0][:, KEY_BLOCK:], cs[1][:, KEY_BLOCK:]], axis=1)
        w = jnp.exp2(ls + later + run_ref[r0:, :])
        if masked:
            w = jnp.where(mask, w, 0.0)
        acc_ref[r0:, :] += jnp.dot(w.astype(BF16), v2, preferred_element_type=F32)
        run_ref[r0:, :] += total

    acc_ref[...] = jnp.zeros_like(acc_ref)
    run_ref[...] = jnp.zeros_like(run_ref)
    for jj in reversed(range(kb_per_tile)):
        block(i * kb_per_tile + jj, jj * KEY_BLOCK, True)

    def body(t, carry):
        for u in range(LOOP_BLOCKS):
            block(i * kb_per_tile - 1 - (t * LOOP_BLOCKS + u), 0, False)
        return carry

    lax.fori_loop(0, i * (kb_per_tile // LOOP_BLOCKS), body, 0)
    o_ref[0] = acc_ref[...].astype(o_ref.dtype)


def _prompt_attn(q, k, v, sb_bias, m2):
    b, t, c = q.shape
    tq = Q_TILE
    hw = 2 * HEAD_DIM
    assert t % tq == 0 and (tq // KEY_BLOCK) % LOOP_BLOCKS == 0
    grid_spec = pltpu.PrefetchScalarGridSpec(
        num_scalar_prefetch=1,
        grid=(b, c // hw, t // tq),
        in_specs=[pl.BlockSpec((1, tq, hw), lambda bb, hp, i, bias: (bb, i, hp)),
                  pl.BlockSpec((1, t, hw), lambda bb, hp, i, bias: (bb, 0, hp)),
                  pl.BlockSpec((1, t, hw), lambda bb, hp, i, bias: (bb, 0, hp)),
                  pl.BlockSpec(m2.shape, lambda bb, hp, i, bias: (0, 0))],
        out_specs=pl.BlockSpec((1, tq, hw), lambda bb, hp, i, bias: (bb, i, hp)),
        scratch_shapes=[pltpu.VMEM((tq, hw), F32), pltpu.VMEM((tq, 2 * KEY_BLOCK), F32)],
    )
    return pl.pallas_call(
        _prompt_attn_kernel,
        grid_spec=grid_spec,
        out_shape=jax.ShapeDtypeStruct((b, t, c), BF16),
        compiler_params=pltpu.CompilerParams(
            dimension_semantics=("parallel", "parallel", "arbitrary"), vmem_limit_bytes=VMEM_LIMIT),
        name="prompt_attn",
    )(sb_bias, q, k, v, m2)


def _sample_attn_parts(q_ref, bias_ref, knew_ref, vnew_ref, m2_ref, k_refs, v_refs, o_ref, acc_ref, run_ref):
    nq = q_ref.shape[1]
    rows = nq * N_HEADS
    c = q_ref.shape[2]
    row = lax.broadcasted_iota(jnp.int32, (rows, c), 0)
    colh = lax.broadcasted_iota(jnp.int32, (rows, c), 1) // HEAD_DIM
    q = q_ref[0].astype(F32)
    q_rows = jnp.concatenate([jnp.broadcast_to(q[t:t + 1], (N_HEADS, c)) for t in range(nq)], axis=0)
    head_sel = (row % N_HEADS) == colh
    qbd = jnp.where(head_sel, q_rows, 0.0)
    bias = bias_ref[...]
    m2 = m2_ref[...]

    def attend(kt, vt, mask):
        n = kt.shape[1] // KEY_BLOCK
        blk = lambda a, p: a[:, p * KEY_BLOCK:(p + 1) * KEY_BLOCK]
        z = jnp.dot(qbd, kt, preferred_element_type=F32) + jnp.concatenate([bias] * n, axis=1)
        ls, lom = _stick_logs(z)
        if mask is not None:
            lom = jnp.where(mask, lom, 0.0)
        hi, lo = _split_bf16(lom)
        lhs = jnp.concatenate([jnp.concatenate([blk(hi, p), blk(lo, p)], axis=1) for p in range(n)], axis=0)
        cs = jnp.dot(lhs, m2, preferred_element_type=F32)
        run = run_ref[...]
        args = []
        for p in range(n):
            csp = cs[p * rows:(p + 1) * rows]
            args.append(blk(ls, p) + csp[:, :KEY_BLOCK] + run)
            run = run + csp[:, KEY_BLOCK:]
        w = jnp.exp2(jnp.concatenate(args, axis=1))
        if mask is not None:
            w = jnp.where(mask, w, 0.0)
        acc_ref[...] += lax.dot_general(w, vt, (((1,), (1,)), ((), ())), preferred_element_type=F32)
        run_ref[...] = run

    def start():
        acc_ref[...] = jnp.zeros_like(acc_ref)
        run_ref[...] = jnp.zeros_like(run_ref)
        key = lax.broadcasted_iota(jnp.int32, (rows, KEY_BLOCK), 1)
        qi = lax.broadcasted_iota(jnp.int32, (rows, KEY_BLOCK), 0) // N_HEADS
        attend(knew_ref[0], vnew_ref[0], key < qi)

    def pages():
        attend(jnp.concatenate([r[0] for r in k_refs], axis=1),
               jnp.concatenate([r[0] for r in v_refs], axis=1), None)

    def finish():
        picked = jnp.where(head_sel, acc_ref[...], 0.0)
        out = [jnp.sum(picked[t * N_HEADS:(t + 1) * N_HEADS], axis=0, keepdims=True) for t in range(nq)]
        o_ref[0] = jnp.concatenate(out, axis=0)

    return start, pages, finish


def _pages_transposed(cache):
    _, n_pool, page, h, hd = cache.shape
    return jnp.transpose(cache, (0, 1, 3, 4, 2)).reshape(n_pool, h * hd, page)


def _conv_kernel(u_ref, halo_ref, w_ref, b_ref, g_ref, beta_ref, s_ref, win_ref, shift_ref):
    tm = u_ref.shape[1]
    first = pl.program_id(1) == 0

    @pl.when(first)
    def _():
        win_ref[0:HALO, :] = jnp.zeros((HALO, win_ref.shape[1]), F32)

    @pl.when(jnp.logical_not(first))
    def _():
        win_ref[0:HALO, :] = halo_ref[0]

    win_ref[HALO:HALO + tm, :] = u_ref[0]
    n = HALO + tm - SUBLANES
    for r in range(1, SUBLANES):
        shift_ref[r - 1, 0:n, :] = win_ref[r:r + n, :]

    def tap_rows(start):
        r = start % SUBLANES
        if r == 0:
            return win_ref[start:start + CONV_ROWS, :]
        return shift_ref[r - 1, start - r:start - r + CONV_ROWS, :]

    lead = HALO - (CONV_W - 1)
    for r0 in range(0, tm, CONV_ROWS):
        acc = jnp.broadcast_to(b_ref[...], (CONV_ROWS, b_ref.shape[1]))
        for tap in range(CONV_W):
            acc = acc + w_ref[tap:tap + 1, :] * tap_rows(r0 + lead + tap)
        y = _ln(acc) * g_ref[...] + beta_ref[...]
        s_ref[0, r0:r0 + CONV_ROWS, :] = _silu(y).astype(s_ref.dtype)


def _conv_module(u, conv_w, conv_b, cln_g, cln_b, tm):
    b, t, c = u.shape
    assert tm % HALO == 0 and tm % CONV_ROWS == 0 and t % tm == 0
    per_tile = tm // HALO
    vec = pl.BlockSpec((1, c), lambda bb, i: (0, 0))
    return pl.pallas_call(
        _conv_kernel,
        grid=(b, t // tm),
        in_specs=[pl.BlockSpec((1, tm, c), lambda bb, i: (bb, i, 0)),
                  pl.BlockSpec((1, HALO, c), lambda bb, i: (bb, jnp.maximum(i * per_tile - 1, 0), 0)),
                  pl.BlockSpec((CONV_W, c), lambda bb, i: (0, 0)), vec, vec, vec],
        out_specs=pl.BlockSpec((1, tm, c), lambda bb, i: (bb, i, 0)),
        out_shape=jax.ShapeDtypeStruct((b, t, c), BF16),
        scratch_shapes=[pltpu.VMEM((HALO + tm, c), F32), pltpu.VMEM((SUBLANES - 1, HALO + tm, c), F32)],
        compiler_params=pltpu.CompilerParams(dimension_semantics=("parallel", "parallel"),
                                             vmem_limit_bytes=VMEM_LIMIT),
        name="conv_module",
    )(u, u, conv_w, conv_b.reshape(1, c), cln_g.reshape(1, c), cln_b.reshape(1, c))


def _step_conv_kernel(hist_ref, u_ref, w_ref, b_ref, g_ref, beta_ref, s_ref):
    n_hist = hist_ref.shape[0]
    for t in range(u_ref.shape[0]):
        acc = jnp.broadcast_to(b_ref[...], u_ref.shape[1:])
        for tap in range(CONV_W):
            r = t + tap + n_hist - (CONV_W - 1)
            rows = hist_ref[r] if r < n_hist else u_ref[r - n_hist]
            acc = acc + w_ref[tap:tap + 1, :] * rows
        y = _ln(acc) * g_ref[...] + beta_ref[...]
        s_ref[t] = _silu(y)


def _step_conv_module(u_t, hist_t, conv_w, conv_b, cln_g, cln_b):
    c = u_t.shape[-1]
    return pl.pallas_call(
        _step_conv_kernel,
        out_shape=jax.ShapeDtypeStruct(u_t.shape, F32),
        compiler_params=pltpu.CompilerParams(vmem_limit_bytes=VMEM_LIMIT),
        name="step_conv_module",
    )(hist_t, u_t, conv_w, conv_b.reshape(1, c), cln_g.reshape(1, c), cln_b.reshape(1, c))


def _post_phases(phase, x_ref, attn_ref, s_ref, gate1_ref, shift2_ref, scale2_ref, gate2_ref,
                 wo_ref, g1_ref, b1_ref, wgu_ref, wdown_ref, g2_ref, b2_ref, y_ref, x1_ref, h2_ref, act_ref,
                 alongside=None):
    c = attn_ref.shape[1]
    d_ff = wdown_ref.shape[0]
    fc = d_ff // 2

    def out_proj():
        mix = jnp.dot(attn_ref[...].astype(BF16), wo_ref[0:c, :], preferred_element_type=F32)
        mix = mix + jnp.dot(s_ref[...].astype(BF16), wo_ref[c:, :], preferred_element_type=F32)
        x1 = _ln(ALPHA * x_ref[...] + (1.0 + gate1_ref[...]) * mix) * g1_ref[...] + b1_ref[...]
        x1_ref[...] = x1
        h2_ref[...] = (_ln(x1) * (1.0 + scale2_ref[...]) + shift2_ref[...]).astype(BF16)

    def swiglu(j):
        h2 = h2_ref[...]
        gt = jnp.dot(h2, wgu_ref[:, j * fc:(j + 1) * fc], preferred_element_type=F32)
        up = jnp.dot(h2, wgu_ref[:, d_ff + j * fc:d_ff + (j + 1) * fc], preferred_element_type=F32)
        act_ref[:, j * fc:(j + 1) * fc] = (_silu(gt) * up).astype(BF16)

    def down_proj():
        f = jnp.dot(act_ref[...], wdown_ref[...], preferred_element_type=F32)
        y_ref[...] = _ln(ALPHA * x1_ref[...] + (1.0 + gate2_ref[...]) * f) * g2_ref[...] + b2_ref[...]

    parts = (out_proj, functools.partial(swiglu, 0), functools.partial(swiglu, 1), down_proj)
    assert len(parts) == POST_PHASES
    for k, part in enumerate(parts):
        if phase is None:
            part()
        else:
            @pl.when(phase == k)
            def _(part=part):
                part()
                if alongside is not None:
                    alongside()


def _post_kernel(*refs):
    _post_phases(None, *refs)


def _post_specs(mod, tm, rows_per_group, d, c, cc, d_ff, tile):
    tok = lambda w: pl.BlockSpec((tm, w), lambda i, *_: (tile(i), 0))
    in_specs = ([tok(d), tok(c), tok(cc)]
                + [_mod_spec(mod, tm, rows_per_group, ch, tile) for ch in (2, 3, 4, 5)]
                + [_const_spec((d, d)), _const_spec((1, d)), _const_spec((1, d)),
                   _const_spec((d, 2 * d_ff)), _const_spec((d_ff, d)), _const_spec((1, d)), _const_spec((1, d))])
    scratch = [pltpu.VMEM((tm, d), F32), pltpu.VMEM((tm, d), BF16), pltpu.VMEM((tm, d_ff), BF16)]
    return in_specs, tok(d), scratch


def _post(x, attn, s, mod, rows_per_group, w_o, ln1_g, ln1_b, w_gu, w_down, ln2_g, ln2_b, tm):
    n, d = x.shape
    d_ff = w_down.shape[0]
    vec = lambda a: a.reshape(1, d)
    in_specs, out_spec, scratch = _post_specs(mod, tm, rows_per_group, d, attn.shape[1], s.shape[1], d_ff, lambda i: i)
    return pl.pallas_call(
        _post_kernel,
        grid=(n // tm,),
        in_specs=in_specs,
        out_specs=out_spec,
        out_shape=jax.ShapeDtypeStruct((n, d), F32),
        scratch_shapes=scratch,
        compiler_params=pltpu.CompilerParams(dimension_semantics=("parallel",),
                                             vmem_limit_bytes=VMEM_LIMIT),
        name="post",
    )(x, attn, s, mod, mod, mod, mod, w_o, vec(ln1_g), vec(ln1_b), w_gu, w_down, vec(ln2_g), vec(ln2_b))


N_POST_INPUTS = 14


def _post_sample_kernel(pages_per_step, steps_per_seq, pt_ref, *refs):
    post_in = refs[:N_POST_INPUTS]
    q_ref, bias_ref, knew_ref, vnew_ref, m2_ref = refs[N_POST_INPUTS:N_POST_INPUTS + 5]
    pages = refs[N_POST_INPUTS + 5:N_POST_INPUTS + 5 + 2 * pages_per_step]
    y_ref, o_ref, x1_ref, h2_ref, act_ref, acc_ref, run_ref = refs[N_POST_INPUTS + 5 + 2 * pages_per_step:]
    i = pl.program_id(0)
    step = i % steps_per_seq
    start, attend_pages, finish = _sample_attn_parts(q_ref, bias_ref, knew_ref, vnew_ref, m2_ref,
                                                     pages[:pages_per_step], pages[pages_per_step:],
                                                     o_ref, acc_ref, run_ref)
    pl.when(step == 0)(start)
    _post_phases(i % POST_PHASES, *post_in, y_ref, x1_ref, h2_ref, act_ref, alongside=attend_pages)
    pl.when(step == steps_per_seq - 1)(finish)


def _post_and_sample_attn(x, attn, s, mod, rows_per_group, w_o, ln1_g, ln1_b, w_gu, w_down, ln2_g, ln2_b, tm,
                          q, k_new, v_new, cache_k, cache_v, page_table, sb_bias, m2, pages_per_step=16):
    n, d = x.shape
    d_ff = w_down.shape[0]
    b, nq, c = q.shape
    n_pages = page_table.shape[1]
    steps_per_seq = n_pages // pages_per_step
    n_steps = b * steps_per_seq
    assert n_steps == (n // tm) * POST_PHASES, "token tiles and page steps must pair up"
    rows = nq * N_HEADS
    bias = jnp.broadcast_to(jnp.tile(sb_bias, nq)[:, None], (rows, KEY_BLOCK)).astype(F32)
    pad = ((0, 0), (0, 0), (0, KEY_BLOCK - nq))
    k_new = jnp.pad(jnp.swapaxes(k_new, 1, 2), pad)
    v_new = jnp.pad(jnp.swapaxes(v_new, 1, 2), pad)

    def page_spec(p):
        def index(i, pt):
            page = n_pages - 1 - ((i % steps_per_seq) * pages_per_step + p)
            return pt[i // steps_per_seq, page], 0, 0
        return pl.BlockSpec((1, c, KEY_BLOCK), index)

    per_seq = lambda shape: pl.BlockSpec(shape, lambda i, pt: (i // steps_per_seq, 0, 0))
    post_specs, y_spec, post_scratch = _post_specs(mod, tm, rows_per_group, d, attn.shape[1], s.shape[1], d_ff,
                                                   lambda i: i // POST_PHASES)
    grid_spec = pltpu.PrefetchScalarGridSpec(
        num_scalar_prefetch=1,
        grid=(n_steps,),
        in_specs=post_specs
                 + [per_seq((1, nq, c)), pl.BlockSpec((rows, KEY_BLOCK), lambda i, pt: (0, 0)),
                    per_seq((1, c, KEY_BLOCK)), per_seq((1, c, KEY_BLOCK)),
                    pl.BlockSpec(m2.shape, lambda i, pt: (0, 0))]
                 + [page_spec(p) for p in range(pages_per_step)] * 2,
        out_specs=[y_spec, per_seq((1, nq, c))],
        scratch_shapes=post_scratch + [pltpu.VMEM((rows, c), F32), pltpu.VMEM((rows, KEY_BLOCK), F32)],
    )
    vec = lambda a: a.reshape(1, d)
    return pl.pallas_call(
        functools.partial(_post_sample_kernel, pages_per_step, steps_per_seq),
        grid_spec=grid_spec,
        out_shape=[jax.ShapeDtypeStruct((n, d), F32), jax.ShapeDtypeStruct((b, nq, c), F32)],
        compiler_params=pltpu.CompilerParams(dimension_semantics=("arbitrary",), vmem_limit_bytes=VMEM_LIMIT),
        name="post_sample_attn",
    )(page_table, x, attn, s, mod, mod, mod, mod, w_o, vec(ln1_g), vec(ln1_b), w_gu, w_down, vec(ln2_g), vec(ln2_b),
      q, bias, k_new, v_new, m2, *([cache_k] * pages_per_step), *([cache_v] * pages_per_step))


def kernel(x_prompt, x_sample, c_prompt, c_sample, cache_k, cache_v, page_table, state_conv, w_ada, b_ada, w_in, sb_bias, conv_w, conv_b, cln_g, cln_b, w_o, ln1_g, ln1_b, w_gu, w_down, ln2_g, ln2_b):
    assert w_in.shape[0] == 1, "single-layer model"
    bp, tp, d = x_prompt.shape
    bs, ts, _ = x_sample.shape
    n_pool, page, _, _ = cache_k.shape[1:]
    c = C_ATTN
    cc = state_conv.shape[-1]
    hist = CONV_W - 1

    w_in_b = w_in[0].astype(BF16)
    w_o_b = w_o[0].astype(BF16)
    w_gu_b = w_gu[0].astype(BF16)
    w_down_b = w_down[0].astype(BF16)
    m2 = _suffix_matrix()
    bias2 = sb_bias[0] * LOG2E

    mod = _ada(jnp.concatenate([c_prompt, c_sample], axis=0), w_ada[0], b_ada[0])
    mod_p = mod[:bp].reshape(bp, 1, 6 * d)
    mod_s = jnp.repeat(mod[bp:], ts, axis=0).reshape(1, bs * ts, 6 * d)

    xp = x_prompt.reshape(bp * tp, d)
    q_p, kt_p, vt_p, kb_p, vb_p, u_p = _inproj(xp, mod_p, tp, w_in_b, tm=512, kv_transposed=True)
    attn_p = _prompt_attn(q_p.reshape(bp, tp, c), kb_p.reshape(bp, tp, c), vb_p.reshape(bp, tp, c),
                          bias2, m2)
    u_p3 = u_p.reshape(bp, tp, cc)
    s_p = _conv_module(u_p3, conv_w[0], conv_b[0], cln_g[0], cln_b[0], tm=256)

    xs = x_sample.reshape(bs * ts, d)
    q_s, k_s, v_s, _, _, u_s = _inproj(xs, mod_s, ts, w_in_b, tm=bs * ts, kv_transposed=False)
    y_p, attn_s = _post_and_sample_attn(
        xp, attn_p.reshape(bp * tp, c), s_p.reshape(bp * tp, cc), mod_p, tp,
        w_o_b, ln1_g[0], ln1_b[0], w_gu_b, w_down_b, ln2_g[0], ln2_b[0], 256,
        q_s.reshape(bs, ts, c), k_s.reshape(bs, ts, c), v_s.reshape(bs, ts, c),
        _pages_transposed(cache_k), _pages_transposed(cache_v), page_table, bias2, m2)
    u_st = jnp.swapaxes(u_s.reshape(bs, ts, cc), 0, 1)
    hist_t = jnp.swapaxes(state_conv[0], 0, 1)
    s_s = jnp.swapaxes(_step_conv_module(u_st, hist_t, conv_w[0], conv_b[0], cln_g[0], cln_b[0]), 0, 1)
    y_s = _post(xs, attn_s.reshape(bs * ts, c), s_s.reshape(bs * ts, cc), mod_s, ts,
                w_o_b, ln1_g[0], ln1_b[0], w_gu_b, w_down_b, ln2_g[0], ln2_b[0], tm=bs * ts)

    heads = lambda a, b, t: a.reshape(1, b, t, N_HEADS, HEAD_DIM)
    heads_t = lambda a: jnp.transpose(a.reshape(1, bp, N_HEADS, HEAD_DIM, tp), (0, 1, 4, 2, 3))
    new_conv_p = u_p3[:, tp - hist:][None]
    new_conv_s = jnp.swapaxes(jnp.concatenate([hist_t[ts:], u_st], axis=0), 0, 1)[None]
    return (y_p.reshape(bp, tp, d), y_s.reshape(bs, ts, d),
            heads_t(kt_p), heads_t(vt_p), new_conv_p,
            heads(k_s, bs, ts), heads(v_s, bs, ts), new_conv_s)
```

```python
import functools

import jax
import jax.numpy as jnp
from jax import lax
from jax.experimental import pallas as pl
from jax.experimental.pallas import tpu as pltpu

F32 = jnp.float32
BF16 = jnp.bfloat16

HEAD_DIM = 64
N_HEADS = 8
C_ATTN = N_HEADS * HEAD_DIM
CONV_W = 31
SUBLANES = 8
HALO = 32
CONV_ROWS = 32
LN_EPS = 1e-5
ALPHA = 2.0 ** 0.25
KEY_BLOCK = 128
LOG2E = 1.4426950408889634
Q_SCALE = HEAD_DIM ** -0.5 * LOG2E
Q_TILE = 512
HEAD_PAIRS = 4
LOOP_BLOCKS = 2
POST_PHASES = 4
VMEM_LIMIT = 56 * 1024 * 1024


def _ln(x):
    mu = jnp.mean(x, axis=-1, keepdims=True)
    xc = x - mu
    var = jnp.mean(xc * xc, axis=-1, keepdims=True)
    return xc * lax.rsqrt(var + LN_EPS)


def _silu(x):
    return x * jax.nn.sigmoid(x)


def _stick_logs(z2):
    ls = jnp.minimum(z2, 0.0) - jnp.log2(1.0 + jnp.exp2(-jnp.abs(z2)))
    return ls, ls - z2


def _suffix_matrix():
    r = lax.broadcasted_iota(jnp.int32, (KEY_BLOCK, 2 * KEY_BLOCK), 0)
    c = lax.broadcasted_iota(jnp.int32, (KEY_BLOCK, 2 * KEY_BLOCK), 1)
    return jnp.where((c >= KEY_BLOCK) | (r > c), 1.0, 0.0).astype(BF16)


def _ada_kernel(c_ref, w_ref, b_ref, o_ref):
    h = _silu(c_ref[...])
    o_ref[...] = jnp.dot(h, w_ref[...], preferred_element_type=F32) + b_ref[...]


def _ada(c, w_ada, b_ada):
    n, d = c.shape
    dn = w_ada.shape[1]
    tn = 1024
    return pl.pallas_call(
        _ada_kernel,
        grid=(dn // tn,),
        in_specs=[pl.BlockSpec((n, d), lambda j: (0, 0)),
                  pl.BlockSpec((d, tn), lambda j: (0, j)),
                  pl.BlockSpec((1, tn), lambda j: (0, j))],
        out_specs=pl.BlockSpec((n, tn), lambda j: (0, j)),
        out_shape=jax.ShapeDtypeStruct((n, dn), F32),
        compiler_params=pltpu.CompilerParams(dimension_semantics=("arbitrary",),
                                             vmem_limit_bytes=VMEM_LIMIT),
        name="ada",
    )(c, w_ada, b_ada.reshape(1, dn))


def _project(x_ref, shift_ref, scale_ref, w_ref):
    c = C_ATTN
    cc = (w_ref.shape[1] - 3 * c) // 2
    h = (_ln(x_ref[...]) * (1.0 + scale_ref[...]) + shift_ref[...]).astype(BF16)
    dot = lambda lo, hi: jnp.dot(h, w_ref[:, lo:hi], preferred_element_type=F32)
    q = dot(0, c) * Q_SCALE
    k = dot(c, 2 * c)
    v = dot(2 * c, 3 * c)
    u = dot(3 * c, 3 * c + cc) * jax.nn.sigmoid(dot(3 * c + cc, 3 * c + 2 * cc))
    return q, k, v, u


def _inproj_step_kernel(x_ref, shift_ref, scale_ref, w_ref, q_ref, k_ref, v_ref, u_ref):
    q, k, v, u = _project(x_ref, shift_ref, scale_ref, w_ref)
    q_ref[...] = q.astype(BF16)
    k_ref[...] = k
    v_ref[...] = v
    u_ref[...] = u


def _conv_rows(win_ref, phase_ref, w_ref, b_ref, g_ref, beta_ref, s_ref, tm):
    n = HALO + tm - SUBLANES
    for r in range(1, SUBLANES):
        phase_ref[r - 1, 0:n, :] = win_ref[r:r + n, :]

    def tap_rows(start):
        r = start % SUBLANES
        if r == 0:
            return win_ref[start:start + CONV_ROWS, :]
        return phase_ref[r - 1, start - r:start - r + CONV_ROWS, :]

    lead = HALO - (CONV_W - 1)
    for r0 in range(0, tm, CONV_ROWS):
        acc = jnp.broadcast_to(b_ref[...], (CONV_ROWS, b_ref.shape[1]))
        for tap in range(CONV_W):
            acc = acc + w_ref[tap:tap + 1, :] * tap_rows(r0 + lead + tap)
        y = _ln(acc) * g_ref[...] + beta_ref[...]
        s_ref[r0:r0 + CONV_ROWS, :] = _silu(y).astype(s_ref.dtype)


def _inproj_seq_kernel(tiles_per_seq, x_ref, shift_ref, scale_ref, w_ref, cw_ref, cb_ref, cg_ref, cbeta_ref,
                       q_ref, kt_ref, vt_ref, kb_ref, vb_ref, s_ref, tail_ref, win_ref, phase_ref):
    i = pl.program_id(0)
    tm = x_ref.shape[0]

    @pl.when(i == 0)
    def _():
        win_ref[...] = jnp.zeros_like(win_ref)

    q, k, v, u = _project(x_ref, shift_ref, scale_ref, w_ref)
    q_ref[...] = q.astype(BF16)
    kb_ref[...] = k.astype(BF16)
    vb_ref[...] = v.astype(BF16)
    kt_ref[0] = k.T
    vt_ref[0] = v.T

    _conv_rows(win_ref, phase_ref, cw_ref, cb_ref, cg_ref, cbeta_ref, s_ref, tm)
    tail = win_ref[tm:tm + HALO, :]
    tail_ref[0] = tail
    win_ref[0:HALO, :] = jnp.where(i % tiles_per_seq == 0, 0.0, tail)
    win_ref[HALO:HALO + tm, :] = u


def _mod_spec(mod, tm, rows_per_group, chunk, tile=lambda i: i):
    d = mod.shape[-1] // 6
    if mod.shape[1] == 1:
        return pl.BlockSpec((None, 1, d), lambda i, *_: ((tile(i) * tm) // rows_per_group, 0, chunk))
    return pl.BlockSpec((None, tm, d), lambda i, *_: (0, tile(i), chunk))


def _const_spec(shape):
    return pl.BlockSpec(shape, lambda *_: (0,) * len(shape), pipeline_mode=pl.Buffered(1))


def _inproj_step(x, mod, rows_per_group, w_in):
    n, d = x.shape
    c = C_ATTN
    cc = (w_in.shape[1] - 3 * c) // 2
    tok = lambda w: pl.BlockSpec((n, w), lambda i: (0, 0))
    sds = jax.ShapeDtypeStruct
    return pl.pallas_call(
        _inproj_step_kernel,
        grid=(1,),
        in_specs=[tok(d), _mod_spec(mod, n, rows_per_group, 0), _mod_spec(mod, n, rows_per_group, 1),
                  _const_spec((d, w_in.shape[1]))],
        out_specs=[tok(c), tok(c), tok(c), tok(cc)],
        out_shape=[sds((n, c), BF16), sds((n, c), F32), sds((n, c), F32), sds((n, cc), F32)],
        compiler_params=pltpu.CompilerParams(dimension_semantics=("arbitrary",), vmem_limit_bytes=VMEM_LIMIT),
        name="inproj_step",
    )(x, mod, mod, w_in)


def _inproj_seq(x, mod, rows_per_group, w_in, conv_w, conv_b, cln_g, cln_b, tm):
    n, d = x.shape
    c = C_ATTN
    cc = (w_in.shape[1] - 3 * c) // 2
    assert rows_per_group % tm == 0 and tm % CONV_ROWS == 0 and tm >= HALO
    tiles = rows_per_group // tm
    n_tiles = n // tm
    n_seq = n // rows_per_group
    cur = lambda i: jnp.minimum(i, n_tiles - 1)
    prev = lambda i: jnp.maximum(i - 1, 0)
    tok = lambda w, tile: pl.BlockSpec((tm, w), lambda i: (tile(i), 0))
    kt_spec = pl.BlockSpec((1, c, tm), lambda i: (cur(i) // tiles, 0, cur(i) % tiles))
    vec = _const_spec((1, cc))
    sds = jax.ShapeDtypeStruct
    return pl.pallas_call(
        functools.partial(_inproj_seq_kernel, tiles),
        grid=(n_tiles + 1,),
        in_specs=[tok(d, cur), _mod_spec(mod, tm, rows_per_group, 0, cur), _mod_spec(mod, tm, rows_per_group, 1, cur),
                  _const_spec((d, w_in.shape[1])), _const_spec((CONV_W, cc)), vec, vec, vec],
        out_specs=[tok(c, cur), kt_spec, kt_spec, tok(c, cur), tok(c, cur), tok(cc, prev),
                   pl.BlockSpec((1, HALO, cc), lambda i: (prev(i) // tiles, 0, 0))],
        out_shape=[sds((n, c), BF16), sds((n_seq, c, rows_per_group), F32), sds((n_seq, c, rows_per_group), F32),
                   sds((n, c), BF16), sds((n, c), BF16), sds((n, cc), BF16), sds((n_seq, HALO, cc), F32)],
        scratch_shapes=[pltpu.VMEM((HALO + tm, cc), F32), pltpu.VMEM((SUBLANES - 1, HALO + tm, cc), F32)],
        compiler_params=pltpu.CompilerParams(dimension_semantics=("arbitrary",), vmem_limit_bytes=VMEM_LIMIT),
        name="inproj_seq",
    )(x, mod, mod, w_in, conv_w, conv_b.reshape(1, cc), cln_g.reshape(1, cc), cln_b.reshape(1, cc))


def _prompt_attn_kernel(bias_ref, q_ref, k_ref, v_ref, m2_ref, o_ref, acc_ref, run_ref):
    hg = pl.program_id(1)
    i = pl.program_id(2)
    tq = q_ref.shape[1]
    kb_per_tile = tq // KEY_BLOCK
    pw = 2 * HEAD_DIM
    m2 = m2_ref[...]
    lane = lax.broadcasted_iota(jnp.int32, (KEY_BLOCK, pw), 1)
    head_a = lane < HEAD_DIM
    col2 = lax.broadcasted_iota(jnp.int32, (1, 2 * KEY_BLOCK), 1)
    bias2 = [jnp.where(col2 < KEY_BLOCK, bias_ref[2 * (HEAD_PAIRS * hg + p)], bias_ref[2 * (HEAD_PAIRS * hg + p) + 1])
             for p in range(HEAD_PAIRS)]

    def block(j, r0, masked):
        start = pl.multiple_of(j * KEY_BLOCK, KEY_BLOCK)
        for p in range(HEAD_PAIRS):
            lanes = slice(p * pw, (p + 1) * pw)
            kb = k_ref[0, pl.ds(start, KEY_BLOCK), lanes]
            vb = v_ref[0, pl.ds(start, KEY_BLOCK), lanes]
            zero = jnp.zeros_like(kb)
            k2 = jnp.concatenate([jnp.where(head_a, kb, zero), jnp.where(head_a, zero, kb)], axis=0)
            v2 = jnp.concatenate([jnp.where(head_a, vb, zero), jnp.where(head_a, zero, vb)], axis=0)
            z = lax.dot_general(q_ref[0, r0:, lanes], k2, (((1,), (1,)), ((), ())),
                                preferred_element_type=F32) + bias2[p]
            ls, lom = _stick_logs(z)
            if masked:
                n = tq - r0
                key = lax.broadcasted_iota(jnp.int32, (n, 2 * KEY_BLOCK), 1) % KEY_BLOCK
                qrow = lax.broadcasted_iota(jnp.int32, (n, 2 * KEY_BLOCK), 0)
                mask = key < qrow
                lom = jnp.where(mask, lom, 0.0)
            lom16 = lom.astype(BF16)
            cs = [jnp.dot(lom16[:, h * KEY_BLOCK:(h + 1) * KEY_BLOCK], m2, preferred_element_type=F32)
                  for h in range(2)]
            later = jnp.concatenate([cs[0][:, :KEY_BLOCK], cs[1][:, :KEY_BLOCK]], axis=1)
            total = jnp.concatenate([cs[0][:, KEY_BLOCK:], cs[1][:, KEY_BLOCK:]], axis=1)
            w = jnp.exp2(ls + later + run_ref[p, r0:, :])
            if masked:
                w = jnp.where(mask, w, 0.0)
            acc_ref[p, r0:, :] += jnp.dot(w.astype(BF16), v2, preferred_element_type=F32)
            run_ref[p, r0:, :] += total

    acc_ref[...] = jnp.zeros_like(acc_ref)
    run_ref[...] = jnp.zeros_like(run_ref)
    for jj in reversed(range(kb_per_tile)):
        block(i * kb_per_tile + jj, jj * KEY_BLOCK, True)

    def body(t, carry):
        for u in range(LOOP_BLOCKS):
            block(i * kb_per_tile - 1 - (t * LOOP_BLOCKS + u), 0, False)
        return carry

    lax.fori_loop(0, i * (kb_per_tile // LOOP_BLOCKS), body, 0)
    for p in range(HEAD_PAIRS):
        o_ref[0, :, p * pw:(p + 1) * pw] = acc_ref[p].astype(o_ref.dtype)


def _prompt_attn(q, k, v, sb_bias, m2):
    b, t, c = q.shape
    tq = Q_TILE
    hw = HEAD_PAIRS * 2 * HEAD_DIM
    assert t % tq == 0 and (tq // KEY_BLOCK) % LOOP_BLOCKS == 0 and c % hw == 0
    grid_spec = pltpu.PrefetchScalarGridSpec(
        num_scalar_prefetch=1,
        grid=(b, c // hw, t // tq),
        in_specs=[pl.BlockSpec((1, tq, hw), lambda bb, hg, i, bias: (bb, i, hg)),
                  pl.BlockSpec((1, t, hw), lambda bb, hg, i, bias: (bb, 0, hg)),
                  pl.BlockSpec((1, t, hw), lambda bb, hg, i, bias: (bb, 0, hg)),
                  pl.BlockSpec(m2.shape, lambda bb, hg, i, bias: (0, 0))],
        out_specs=pl.BlockSpec((1, tq, hw), lambda bb, hg, i, bias: (bb, i, hg)),
        scratch_shapes=[pltpu.VMEM((HEAD_PAIRS, tq, 2 * HEAD_DIM), F32),
                        pltpu.VMEM((HEAD_PAIRS, tq, 2 * KEY_BLOCK), F32)],
    )
    return pl.pallas_call(
        _prompt_attn_kernel,
        grid_spec=grid_spec,
        out_shape=jax.ShapeDtypeStruct((b, t, c), BF16),
        compiler_params=pltpu.CompilerParams(
            dimension_semantics=("parallel", "parallel", "arbitrary"), vmem_limit_bytes=VMEM_LIMIT),
        name="prompt_attn",
    )(sb_bias, q, k, v, m2)


def _sample_attn_parts(q_ref, bias_ref, knew_ref, vnew_ref, m2_ref, k_refs, v_refs, o_ref, acc_ref, run_ref):
    nq = q_ref.shape[1]
    rows = nq * N_HEADS
    c = q_ref.shape[2]
    row = lax.broadcasted_iota(jnp.int32, (rows, c), 0)
    colh = lax.broadcasted_iota(jnp.int32, (rows, c), 1) // HEAD_DIM
    q = q_ref[0].astype(F32)
    q_rows = jnp.concatenate([jnp.broadcast_to(q[t:t + 1], (N_HEADS, c)) for t in range(nq)], axis=0)
    head_sel = (row % N_HEADS) == colh
    qbd = jnp.where(head_sel, q_rows, 0.0)
    bias = bias_ref[...]
    m2 = m2_ref[...]

    def attend(kt, vt, mask):
        n = kt.shape[1] // KEY_BLOCK
        blk = lambda a, p: a[:, p * KEY_BLOCK:(p + 1) * KEY_BLOCK]
        z = jnp.dot(qbd, kt, preferred_element_type=F32) + jnp.concatenate([bias] * n, axis=1)
        ls, lom = _stick_logs(z)
        if mask is not None:
            lom = jnp.where(mask, lom, 0.0)
        lom16 = lom.astype(BF16)
        lhs = jnp.concatenate([blk(lom16, p) for p in range(n)], axis=0)
        cs = jnp.dot(lhs, m2, preferred_element_type=F32)
        run = run_ref[...]
        args = []
        for p in range(n):
            csp = cs[p * rows:(p + 1) * rows]
            args.append(blk(ls, p) + csp[:, :KEY_BLOCK] + run)
            run = run + csp[:, KEY_BLOCK:]
        w = jnp.exp2(jnp.concatenate(args, axis=1))
        if mask is not None:
            w = jnp.where(mask, w, 0.0)
        acc_ref[...] += lax.dot_general(w, vt, (((1,), (1,)), ((), ())), preferred_element_type=F32)
        run_ref[...] = run

    def start():
        acc_ref[...] = jnp.zeros_like(acc_ref)
        run_ref[...] = jnp.zeros_like(run_ref)
        key = lax.broadcasted_iota(jnp.int32, (rows, KEY_BLOCK), 1)
        qi = lax.broadcasted_iota(jnp.int32, (rows, KEY_BLOCK), 0) // N_HEADS
        attend(knew_ref[0], vnew_ref[0], key < qi)

    def pages():
        attend(jnp.concatenate([r[...] for r in k_refs], axis=1),
               jnp.concatenate([r[...] for r in v_refs], axis=1), None)

    def finish():
        picked = jnp.where(head_sel, acc_ref[...], 0.0)
        out = [jnp.sum(picked[t * N_HEADS:(t + 1) * N_HEADS], axis=0, keepdims=True) for t in range(nq)]
        o_ref[0] = jnp.concatenate(out, axis=0)

    return start, pages, finish


def _pages_transposed(cache):
    _, n_pool, page, h, hd = cache.shape
    return jnp.transpose(cache, (0, 1, 3, 4, 2)).reshape(n_pool, h * hd, page)


def _step_conv_kernel(hist_ref, u_ref, w_ref, b_ref, g_ref, beta_ref, s_ref):
    n_hist = hist_ref.shape[0]
    for t in range(u_ref.shape[0]):
        acc = jnp.broadcast_to(b_ref[...], u_ref.shape[1:])
        for tap in range(CONV_W):
            r = t + tap + n_hist - (CONV_W - 1)
            rows = hist_ref[r] if r < n_hist else u_ref[r - n_hist]
            acc = acc + w_ref[tap:tap + 1, :] * rows
        y = _ln(acc) * g_ref[...] + beta_ref[...]
        s_ref[t] = _silu(y)


def _step_conv_module(u_t, hist_t, conv_w, conv_b, cln_g, cln_b):
    c = u_t.shape[-1]
    return pl.pallas_call(
        _step_conv_kernel,
        out_shape=jax.ShapeDtypeStruct(u_t.shape, F32),
        compiler_params=pltpu.CompilerParams(vmem_limit_bytes=VMEM_LIMIT),
        name="step_conv_module",
    )(hist_t, u_t, conv_w, conv_b.reshape(1, c), cln_g.reshape(1, c), cln_b.reshape(1, c))


def _post_phases(phase, x_ref, attn_ref, s_ref, gate1_ref, shift2_ref, scale2_ref, gate2_ref,
                 wo_ref, g1_ref, b1_ref, wgu_ref, wdown_ref, g2_ref, b2_ref, y_ref, x1_ref, h2_ref, act_ref,
                 alongside=None):
    c = attn_ref.shape[1]
    d_ff = wdown_ref.shape[0]
    fc = d_ff // 2

    def out_proj():
        mix = jnp.dot(attn_ref[...].astype(BF16), wo_ref[0:c, :], preferred_element_type=F32)
        mix = mix + jnp.dot(s_ref[...].astype(BF16), wo_ref[c:, :], preferred_element_type=F32)
        x1 = _ln(ALPHA * x_ref[...] + (1.0 + gate1_ref[...]) * mix) * g1_ref[...] + b1_ref[...]
        x1_ref[...] = x1
        h2_ref[...] = (_ln(x1) * (1.0 + scale2_ref[...]) + shift2_ref[...]).astype(BF16)

    def swiglu(j):
        h2 = h2_ref[...]
        gt = jnp.dot(h2, wgu_ref[:, j * fc:(j + 1) * fc], preferred_element_type=F32)
        up = jnp.dot(h2, wgu_ref[:, d_ff + j * fc:d_ff + (j + 1) * fc], preferred_element_type=F32)
        act_ref[:, j * fc:(j + 1) * fc] = (_silu(gt) * up).astype(BF16)

    def down_proj():
        f = jnp.dot(act_ref[...], wdown_ref[...], preferred_element_type=F32)
        y_ref[...] = _ln(ALPHA * x1_ref[...] + (1.0 + gate2_ref[...]) * f) * g2_ref[...] + b2_ref[...]

    parts = (out_proj, functools.partial(swiglu, 0), functools.partial(swiglu, 1), down_proj)
    assert len(parts) == POST_PHASES
    for k, part in enumerate(parts):
        if phase is None:
            part()
        else:
            @pl.when(phase == k)
            def _(part=part):
                part()
                if alongside is not None:
                    alongside()


def _post_kernel(*refs):
    _post_phases(None, *refs)


def _post_specs(mod, tm, rows_per_group, d, c, cc, d_ff, tile):
    tok = lambda w: pl.BlockSpec((tm, w), lambda i, *_: (tile(i), 0))
    in_specs = ([tok(d), tok(c), tok(cc)]
                + [_mod_spec(mod, tm, rows_per_group, ch, tile) for ch in (2, 3, 4, 5)]
                + [_const_spec((d, d)), _const_spec((1, d)), _const_spec((1, d)),
                   _const_spec((d, 2 * d_ff)), _const_spec((d_ff, d)), _const_spec((1, d)), _const_spec((1, d))])
    scratch = [pltpu.VMEM((tm, d), F32), pltpu.VMEM((tm, d), BF16), pltpu.VMEM((tm, d_ff), BF16)]
    return in_specs, tok(d), scratch


def _post(x, attn, s, mod, rows_per_group, w_o, ln1_g, ln1_b, w_gu, w_down, ln2_g, ln2_b, tm):
    n, d = x.shape
    d_ff = w_down.shape[0]
    vec = lambda a: a.reshape(1, d)
    in_specs, out_spec, scratch = _post_specs(mod, tm, rows_per_group, d, attn.shape[1], s.shape[1], d_ff, lambda i: i)
    return pl.pallas_call(
        _post_kernel,
        grid=(n // tm,),
        in_specs=in_specs,
        out_specs=out_spec,
        out_shape=jax.ShapeDtypeStruct((n, d), F32),
        scratch_shapes=scratch,
        compiler_params=pltpu.CompilerParams(dimension_semantics=("parallel",),
                                             vmem_limit_bytes=VMEM_LIMIT),
        name="post",
    )(x, attn, s, mod, mod, mod, mod, w_o, vec(ln1_g), vec(ln1_b), w_gu, w_down, vec(ln2_g), vec(ln2_b))


N_POST_INPUTS = 14


def _post_sample_kernel(pages_per_step, steps_per_seq, pt_ref, *refs):
    post_in = refs[:N_POST_INPUTS]
    q_ref, bias_ref, knew_ref, vnew_ref, m2_ref, kcache_ref, vcache_ref = refs[N_POST_INPUTS:N_POST_INPUTS + 7]
    y_ref, o_ref, x1_ref, h2_ref, act_ref, acc_ref, run_ref, kbuf, vbuf, sems = refs[N_POST_INPUTS + 7:]
    i = pl.program_id(0)
    n_steps = pl.num_programs(0)
    n_pages = steps_per_seq * pages_per_step
    step = i % steps_per_seq
    slot = i % 2

    def page_copies(j, half):
        seq = j // steps_per_seq
        latest = n_pages - 1 - (j % steps_per_seq) * pages_per_step
        copies = []
        for p in range(pages_per_step):
            page = pt_ref[seq, latest - p]
            copies.append(pltpu.make_async_copy(kcache_ref.at[page], kbuf.at[half, p], sems.at[0, half]))
            copies.append(pltpu.make_async_copy(vcache_ref.at[page], vbuf.at[half, p], sems.at[1, half]))
        return copies

    @pl.when(i == 0)
    def _():
        for copy in page_copies(0, 0):
            copy.start()

    @pl.when(i + 1 < n_steps)
    def _():
        for copy in page_copies(i + 1, 1 - slot):
            copy.start()

    for copy in page_copies(i, slot):
        copy.wait()

    k_pages = [kbuf.at[slot, p] for p in range(pages_per_step)]
    v_pages = [vbuf.at[slot, p] for p in range(pages_per_step)]
    start, attend_pages, finish = _sample_attn_parts(q_ref, bias_ref, knew_ref, vnew_ref, m2_ref,
                                                     k_pages, v_pages, o_ref, acc_ref, run_ref)
    pl.when(step == 0)(start)
    _post_phases(i % POST_PHASES, *post_in, y_ref, x1_ref, h2_ref, act_ref, alongside=attend_pages)
    pl.when(step == steps_per_seq - 1)(finish)


def _post_and_sample_attn(x, attn, s, mod, rows_per_group, w_o, ln1_g, ln1_b, w_gu, w_down, ln2_g, ln2_b, tm,
                          q, k_new, v_new, cache_k, cache_v, page_table, sb_bias, m2, pages_per_step=16):
    n, d = x.shape
    d_ff = w_down.shape[0]
    b, nq, c = q.shape
    n_pages = page_table.shape[1]
    steps_per_seq = n_pages // pages_per_step
    n_steps = b * steps_per_seq
    assert n_steps == (n // tm) * POST_PHASES, "token tiles and page steps must pair up"
    rows = nq * N_HEADS
    bias = jnp.broadcast_to(jnp.tile(sb_bias, nq)[:, None], (rows, KEY_BLOCK)).astype(F32)
    pad = ((0, 0), (0, 0), (0, KEY_BLOCK - nq))
    k_new = jnp.pad(jnp.swapaxes(k_new, 1, 2), pad)
    v_new = jnp.pad(jnp.swapaxes(v_new, 1, 2), pad)

    per_seq = lambda shape: pl.BlockSpec(shape, lambda i, pt: (i // steps_per_seq, 0, 0))
    post_specs, y_spec, post_scratch = _post_specs(mod, tm, rows_per_group, d, attn.shape[1], s.shape[1], d_ff,
                                                   lambda i: i // POST_PHASES)
    page_buf = pltpu.VMEM((2, pages_per_step, c, KEY_BLOCK), cache_k.dtype)
    grid_spec = pltpu.PrefetchScalarGridSpec(
        num_scalar_prefetch=1,
        grid=(n_steps,),
        in_specs=post_specs
                 + [per_seq((1, nq, c)), pl.BlockSpec((rows, KEY_BLOCK), lambda i, pt: (0, 0)),
                    per_seq((1, c, KEY_BLOCK)), per_seq((1, c, KEY_BLOCK)),
                    pl.BlockSpec(m2.shape, lambda i, pt: (0, 0)),
                    pl.BlockSpec(memory_space=pl.ANY), pl.BlockSpec(memory_space=pl.ANY)],
        out_specs=[y_spec, per_seq((1, nq, c))],
        scratch_shapes=post_scratch + [pltpu.VMEM((rows, c), F32), pltpu.VMEM((rows, KEY_BLOCK), F32),
                                       page_buf, page_buf, pltpu.SemaphoreType.DMA((2, 2))],
    )
    vec = lambda a: a.reshape(1, d)
    return pl.pallas_call(
        functools.partial(_post_sample_kernel, pages_per_step, steps_per_seq),
        grid_spec=grid_spec,
        out_shape=[jax.ShapeDtypeStruct((n, d), F32), jax.ShapeDtypeStruct((b, nq, c), F32)],
        compiler_params=pltpu.CompilerParams(dimension_semantics=("arbitrary",), vmem_limit_bytes=VMEM_LIMIT),
        name="post_sample_attn",
    )(page_table, x, attn, s, mod, mod, mod, mod, w_o, vec(ln1_g), vec(ln1_b), w_gu, w_down, vec(ln2_g), vec(ln2_b),
      q, bias, k_new, v_new, m2, cache_k, cache_v)


def kernel(x_prompt, x_sample, c_prompt, c_sample, cache_k, cache_v, page_table, state_conv, w_ada, b_ada, w_in, sb_bias, conv_w, conv_b, cln_g, cln_b, w_o, ln1_g, ln1_b, w_gu, w_down, ln2_g, ln2_b):
    assert w_in.shape[0] == 1, "single-layer model"
    bp, tp, d = x_prompt.shape
    bs, ts, _ = x_sample.shape
    c = C_ATTN
    cc = state_conv.shape[-1]
    hist = CONV_W - 1

    w_in_b = w_in[0].astype(BF16)
    w_o_b = w_o[0].astype(BF16)
    w_gu_b = w_gu[0].astype(BF16)
    w_down_b = w_down[0].astype(BF16)
    m2 = _suffix_matrix()
    bias2 = sb_bias[0] * LOG2E

    mod = _ada(jnp.concatenate([c_prompt, c_sample], axis=0), w_ada[0], b_ada[0])
    mod_p = mod[:bp].reshape(bp, 1, 6 * d)
    mod_s = jnp.repeat(mod[bp:], ts, axis=0).reshape(1, bs * ts, 6 * d)

    xp = x_prompt.reshape(bp * tp, d)
    q_p, kt_p, vt_p, kb_p, vb_p, s_p, u_tail = _inproj_seq(xp, mod_p, tp, w_in_b, conv_w[0], conv_b[0],
                                                           cln_g[0], cln_b[0], tm=512)
    attn_p = _prompt_attn(q_p.reshape(bp, tp, c), kb_p.reshape(bp, tp, c), vb_p.reshape(bp, tp, c),
                          bias2, m2)

    xs = x_sample.reshape(bs * ts, d)
    q_s, k_s, v_s, u_s = _inproj_step(xs, mod_s, ts, w_in_b)
    y_p, attn_s = _post_and_sample_attn(
        xp, attn_p.reshape(bp * tp, c), s_p, mod_p, tp,
        w_o_b, ln1_g[0], ln1_b[0], w_gu_b, w_down_b, ln2_g[0], ln2_b[0], 256,
        q_s.reshape(bs, ts, c), k_s.reshape(bs, ts, c), v_s.reshape(bs, ts, c),
        _pages_transposed(cache_k), _pages_transposed(cache_v), page_table, bias2, m2)
    u_st = jnp.swapaxes(u_s.reshape(bs, ts, cc), 0, 1)
    hist_t = jnp.swapaxes(state_conv[0], 0, 1)
    s_s = jnp.swapaxes(_step_conv_module(u_st, hist_t, conv_w[0], conv_b[0], cln_g[0], cln_b[0]), 0, 1)
    y_s = _post(xs, attn_s.reshape(bs * ts, c), s_s.reshape(bs * ts, cc), mod_s, ts,
                w_o_b, ln1_g[0], ln1_b[0], w_gu_b, w_down_b, ln2_g[0], ln2_b[0], tm=bs * ts)

    heads = lambda a, b, t: a.reshape(1, b, t, N_HEADS, HEAD_DIM)
    heads_t = lambda a: jnp.transpose(a.reshape(1, bp, N_HEADS, HEAD_DIM, tp), (0, 1, 4, 2, 3))
    new_conv_p = u_tail[:, HALO - hist:][None]
    new_conv_s = jnp.swapaxes(jnp.concatenate([hist_t[ts:], u_st], axis=0), 0, 1)[None]
    return (y_p.reshape(bp, tp, d), y_s.reshape(bs, ts, d),
            heads_t(kt_p), heads_t(vt_p), new_conv_p,
            heads(k_s, bs, ts), heads(v_s, bs, ts), new_conv_s)
```

```python
import functools

import jax
import jax.numpy as jnp
from jax import lax
from jax.experimental import pallas as pl
from jax.experimental.pallas import tpu as pltpu

F32 = jnp.float32
BF16 = jnp.bfloat16

HEAD_DIM = 64
N_HEADS = 8
C_ATTN = N_HEADS * HEAD_DIM
CONV_W = 31
SUBLANES = 8
HALO = 32
CONV_ROWS = 512
LN_EPS = 1e-5
ALPHA = 2.0 ** 0.25
KEY_BLOCK = 128
LOG2E = 1.4426950408889634
Q_SCALE = HEAD_DIM ** -0.5 * LOG2E
Q_TILE = 512
HEAD_PAIRS = 4
LOOP_BLOCKS = 4
POST_PHASES = 4
PAGE_SLOTS = 3
VMEM_LIMIT = 56 * 1024 * 1024


def _ln(x):
    mu = jnp.mean(x, axis=-1, keepdims=True)
    xc = x - mu
    var = jnp.mean(xc * xc, axis=-1, keepdims=True)
    return xc * lax.rsqrt(var + LN_EPS)


def _silu(x):
    return x * jax.nn.sigmoid(x)


def _stick_logs(z2):
    ls = jnp.minimum(z2, 0.0) - jnp.log2(1.0 + jnp.exp2(-jnp.abs(z2)))
    return ls, ls - z2


def _suffix_matrix():
    r = lax.broadcasted_iota(jnp.int32, (KEY_BLOCK, 2 * KEY_BLOCK), 0)
    c = lax.broadcasted_iota(jnp.int32, (KEY_BLOCK, 2 * KEY_BLOCK), 1)
    return jnp.where((c >= KEY_BLOCK) | (r > c), 1.0, 0.0).astype(BF16)


def _ada_kernel(c_ref, w_ref, b_ref, o_ref):
    h = _silu(c_ref[...])
    o_ref[...] = jnp.dot(h, w_ref[...], preferred_element_type=F32) + b_ref[...]


def _ada(c, w_ada, b_ada):
    n, d = c.shape
    dn = w_ada.shape[1]
    tn = 1024
    return pl.pallas_call(
        _ada_kernel,
        grid=(dn // tn,),
        in_specs=[pl.BlockSpec((n, d), lambda j: (0, 0)),
                  pl.BlockSpec((d, tn), lambda j: (0, j)),
                  pl.BlockSpec((1, tn), lambda j: (0, j))],
        out_specs=pl.BlockSpec((n, tn), lambda j: (0, j)),
        out_shape=jax.ShapeDtypeStruct((n, dn), F32),
        compiler_params=pltpu.CompilerParams(dimension_semantics=("arbitrary",),
                                             vmem_limit_bytes=VMEM_LIMIT),
        name="ada",
    )(c, w_ada, b_ada.reshape(1, dn))


def _project(x_ref, shift_ref, scale_ref, w_ref):
    c = C_ATTN
    cc = (w_ref.shape[1] - 3 * c) // 2
    h = (_ln(x_ref[...]) * (1.0 + scale_ref[...]) + shift_ref[...]).astype(BF16)
    dot = lambda lo, hi: jnp.dot(h, w_ref[:, lo:hi], preferred_element_type=F32)
    q = dot(0, c) * Q_SCALE
    k = dot(c, 2 * c)
    v = dot(2 * c, 3 * c)
    u = dot(3 * c, 3 * c + cc) * jax.nn.sigmoid(dot(3 * c + cc, 3 * c + 2 * cc))
    return q, k, v, u


def _inproj_step_kernel(x_ref, shift_ref, scale_ref, w_ref, q_ref, k_ref, v_ref, u_ref):
    q, k, v, u = _project(x_ref, shift_ref, scale_ref, w_ref)
    q_ref[...] = q.astype(BF16)
    k_ref[...] = k
    v_ref[...] = v
    u_ref[...] = u


def _conv_rows(win_ref, phase_ref, w_ref, b_ref, g_ref, beta_ref, s_ref, tm):
    n = HALO + tm - SUBLANES
    for r in range(1, SUBLANES):
        phase_ref[r - 1, 0:n, :] = win_ref[r:r + n, :]

    def tap_rows(start):
        r = start % SUBLANES
        if r == 0:
            return win_ref[start:start + CONV_ROWS, :]
        return phase_ref[r - 1, start - r:start - r + CONV_ROWS, :]

    lead = HALO - (CONV_W - 1)
    for r0 in range(0, tm, CONV_ROWS):
        acc = jnp.broadcast_to(b_ref[...], (CONV_ROWS, b_ref.shape[1]))
        for tap in range(CONV_W):
            acc = acc + w_ref[tap:tap + 1, :] * tap_rows(r0 + lead + tap)
        y = _ln(acc) * g_ref[...] + beta_ref[...]
        s_ref[r0:r0 + CONV_ROWS, :] = _silu(y).astype(s_ref.dtype)


def _inproj_seq_kernel(tiles_per_seq, x_ref, shift_ref, scale_ref, w_ref, cw_ref, cb_ref, cg_ref, cbeta_ref,
                       q_ref, kt_ref, vt_ref, kb_ref, vb_ref, s_ref, tail_ref, win_ref, phase_ref):
    i = pl.program_id(0)
    tm = x_ref.shape[0]

    @pl.when(i == 0)
    def _():
        win_ref[...] = jnp.zeros_like(win_ref)

    q, k, v, u = _project(x_ref, shift_ref, scale_ref, w_ref)
    q_ref[...] = q.astype(BF16)
    kb_ref[...] = k.astype(BF16)
    vb_ref[...] = v.astype(BF16)
    kt_ref[0] = k.T
    vt_ref[0] = v.T

    _conv_rows(win_ref, phase_ref, cw_ref, cb_ref, cg_ref, cbeta_ref, s_ref, tm)
    tail = win_ref[tm:tm + HALO, :]
    tail_ref[0] = tail
    win_ref[0:HALO, :] = jnp.where(i % tiles_per_seq == 0, 0.0, tail)
    win_ref[HALO:HALO + tm, :] = u


def _mod_spec(mod, tm, rows_per_group, chunk, tile=lambda i: i):
    d = mod.shape[-1] // 6
    if mod.shape[1] == 1:
        return pl.BlockSpec((None, 1, d), lambda i, *_: ((tile(i) * tm) // rows_per_group, 0, chunk))
    return pl.BlockSpec((None, tm, d), lambda i, *_: (0, tile(i), chunk))


def _const_spec(shape):
    return pl.BlockSpec(shape, lambda *_: (0,) * len(shape), pipeline_mode=pl.Buffered(1))


def _inproj_step(x, mod, rows_per_group, w_in):
    n, d = x.shape
    c = C_ATTN
    cc = (w_in.shape[1] - 3 * c) // 2
    tok = lambda w: pl.BlockSpec((n, w), lambda i: (0, 0))
    sds = jax.ShapeDtypeStruct
    return pl.pallas_call(
        _inproj_step_kernel,
        grid=(1,),
        in_specs=[tok(d), _mod_spec(mod, n, rows_per_group, 0), _mod_spec(mod, n, rows_per_group, 1),
                  _const_spec((d, w_in.shape[1]))],
        out_specs=[tok(c), tok(c), tok(c), tok(cc)],
        out_shape=[sds((n, c), BF16), sds((n, c), F32), sds((n, c), F32), sds((n, cc), F32)],
        compiler_params=pltpu.CompilerParams(dimension_semantics=("arbitrary",), vmem_limit_bytes=VMEM_LIMIT),
        name="inproj_step",
    )(x, mod, mod, w_in)


def _inproj_seq(x, mod, rows_per_group, w_in, conv_w, conv_b, cln_g, cln_b, tm):
    n, d = x.shape
    c = C_ATTN
    cc = (w_in.shape[1] - 3 * c) // 2
    assert rows_per_group % tm == 0 and tm % CONV_ROWS == 0 and tm >= HALO
    tiles = rows_per_group // tm
    n_tiles = n // tm
    n_seq = n // rows_per_group
    cur = lambda i: jnp.minimum(i, n_tiles - 1)
    prev = lambda i: jnp.maximum(i - 1, 0)
    tok = lambda w, tile: pl.BlockSpec((tm, w), lambda i: (tile(i), 0))
    kt_spec = pl.BlockSpec((1, c, tm), lambda i: (cur(i) // tiles, 0, cur(i) % tiles))
    vec = _const_spec((1, cc))
    sds = jax.ShapeDtypeStruct
    return pl.pallas_call(
        functools.partial(_inproj_seq_kernel, tiles),
        grid=(n_tiles + 1,),
        in_specs=[tok(d, cur), _mod_spec(mod, tm, rows_per_group, 0, cur), _mod_spec(mod, tm, rows_per_group, 1, cur),
                  _const_spec((d, w_in.shape[1])), _const_spec((CONV_W, cc)), vec, vec, vec],
        out_specs=[tok(c, cur), kt_spec, kt_spec, tok(c, cur), tok(c, cur), tok(cc, prev),
                   pl.BlockSpec((1, HALO, cc), lambda i: (prev(i) // tiles, 0, 0))],
        out_shape=[sds((n, c), BF16), sds((n_seq, c, rows_per_group), F32), sds((n_seq, c, rows_per_group), F32),
                   sds((n, c), BF16), sds((n, c), BF16), sds((n, cc), BF16), sds((n_seq, HALO, cc), F32)],
        scratch_shapes=[pltpu.VMEM((HALO + tm, cc), F32), pltpu.VMEM((SUBLANES - 1, HALO + tm, cc), F32)],
        compiler_params=pltpu.CompilerParams(dimension_semantics=("arbitrary",), vmem_limit_bytes=VMEM_LIMIT),
        name="inproj_seq",
    )(x, mod, mod, w_in, conv_w, conv_b.reshape(1, cc), cln_g.reshape(1, cc), cln_b.reshape(1, cc))


def _prompt_attn_kernel(bias_ref, q_ref, k_ref, v_ref, m2_ref, o_ref, acc_ref, run_ref):
    hg = pl.program_id(1)
    i = pl.program_id(2)
    tq = q_ref.shape[1]
    kb_per_tile = tq // KEY_BLOCK
    pw = 2 * HEAD_DIM
    m2 = m2_ref[...]
    lane = lax.broadcasted_iota(jnp.int32, (KEY_BLOCK, pw), 1)
    head_a = lane < HEAD_DIM
    col2 = lax.broadcasted_iota(jnp.int32, (1, 2 * KEY_BLOCK), 1)
    bias2 = [jnp.where(col2 < KEY_BLOCK, bias_ref[2 * (HEAD_PAIRS * hg + p)], bias_ref[2 * (HEAD_PAIRS * hg + p) + 1])
             for p in range(HEAD_PAIRS)]

    def block(j, r0, masked):
        start = pl.multiple_of(j * KEY_BLOCK, KEY_BLOCK)
        for p in range(HEAD_PAIRS):
            lanes = slice(p * pw, (p + 1) * pw)
            kb = k_ref[0, pl.ds(start, KEY_BLOCK), lanes]
            vb = v_ref[0, pl.ds(start, KEY_BLOCK), lanes]
            zero = jnp.zeros_like(kb)
            k2 = jnp.concatenate([jnp.where(head_a, kb, zero), jnp.where(head_a, zero, kb)], axis=0)
            v2 = jnp.concatenate([jnp.where(head_a, vb, zero), jnp.where(head_a, zero, vb)], axis=0)
            z = lax.dot_general(q_ref[0, r0:, lanes], k2, (((1,), (1,)), ((), ())),
                                preferred_element_type=F32) + bias2[p]
            ls, lom = _stick_logs(z)
            if masked:
                n = tq - r0
                key = lax.broadcasted_iota(jnp.int32, (n, 2 * KEY_BLOCK), 1) % KEY_BLOCK
                qrow = lax.broadcasted_iota(jnp.int32, (n, 2 * KEY_BLOCK), 0)
                mask = key < qrow
                lom = jnp.where(mask, lom, 0.0)
            lom16 = lom.astype(BF16)
            cs = [jnp.dot(lom16[:, h * KEY_BLOCK:(h + 1) * KEY_BLOCK], m2, preferred_element_type=F32)
                  for h in range(2)]
            later = jnp.concatenate([cs[0][:, :KEY_BLOCK], cs[1][:, :KEY_BLOCK]], axis=1)
            total = jnp.concatenate([cs[0][:, KEY_BLOCK:], cs[1][:, KEY_BLOCK:]], axis=1)
            w = jnp.exp2(ls + later + run_ref[p, r0:, :])
            if masked:
                w = jnp.where(mask, w, 0.0)
            acc_ref[p, r0:, :] += jnp.dot(w.astype(BF16), v2, preferred_element_type=F32)
            run_ref[p, r0:, :] += total

    acc_ref[...] = jnp.zeros_like(acc_ref)
    run_ref[...] = jnp.zeros_like(run_ref)
    for jj in reversed(range(kb_per_tile)):
        block(i * kb_per_tile + jj, jj * KEY_BLOCK, True)

    def body(t, carry):
        for u in range(LOOP_BLOCKS):
            block(i * kb_per_tile - 1 - (t * LOOP_BLOCKS + u), 0, False)
        return carry

    lax.fori_loop(0, i * (kb_per_tile // LOOP_BLOCKS), body, 0)
    for p in range(HEAD_PAIRS):
        o_ref[0, :, p * pw:(p + 1) * pw] = acc_ref[p].astype(o_ref.dtype)


def _prompt_attn(q, k, v, sb_bias, m2):
    b, t, c = q.shape
    tq = Q_TILE
    hw = HEAD_PAIRS * 2 * HEAD_DIM
    assert t % tq == 0 and (tq // KEY_BLOCK) % LOOP_BLOCKS == 0 and c % hw == 0
    grid_spec = pltpu.PrefetchScalarGridSpec(
        num_scalar_prefetch=1,
        grid=(b, c // hw, t // tq),
        in_specs=[pl.BlockSpec((1, tq, hw), lambda bb, hg, i, bias: (bb, i, hg)),
                  pl.BlockSpec((1, t, hw), lambda bb, hg, i, bias: (bb, 0, hg)),
                  pl.BlockSpec((1, t, hw), lambda bb, hg, i, bias: (bb, 0, hg)),
                  pl.BlockSpec(m2.shape, lambda bb, hg, i, bias: (0, 0))],
        out_specs=pl.BlockSpec((1, tq, hw), lambda bb, hg, i, bias: (bb, i, hg)),
        scratch_shapes=[pltpu.VMEM((HEAD_PAIRS, tq, 2 * HEAD_DIM), F32),
                        pltpu.VMEM((HEAD_PAIRS, tq, 2 * KEY_BLOCK), F32)],
    )
    return pl.pallas_call(
        _prompt_attn_kernel,
        grid_spec=grid_spec,
        out_shape=jax.ShapeDtypeStruct((b, t, c), BF16),
        compiler_params=pltpu.CompilerParams(
            dimension_semantics=("parallel", "parallel", "arbitrary"), vmem_limit_bytes=VMEM_LIMIT),
        name="prompt_attn",
    )(sb_bias, q, k, v, m2)


def _sample_attn_parts(q_ref, bias_ref, knew_ref, vnew_ref, m2_ref, k_refs, v_refs, o_ref, acc_ref, run_ref):
    nq = q_ref.shape[1]
    rows = nq * N_HEADS
    c = q_ref.shape[2]
    row = lax.broadcasted_iota(jnp.int32, (rows, c), 0)
    colh = lax.broadcasted_iota(jnp.int32, (rows, c), 1) // HEAD_DIM
    q = q_ref[0].astype(F32)
    q_rows = jnp.concatenate([jnp.broadcast_to(q[t:t + 1], (N_HEADS, c)) for t in range(nq)], axis=0)
    head_sel = (row % N_HEADS) == colh
    qbd = jnp.where(head_sel, q_rows, 0.0)
    bias = bias_ref[...]
    m2 = m2_ref[...]

    nt = (((1,), (1,)), ((), ()))

    def attend(kt, vt, mask, keys_major=False):
        n = kt.shape[0 if keys_major else 1] // KEY_BLOCK
        blk = lambda a, p: a[:, p * KEY_BLOCK:(p + 1) * KEY_BLOCK]
        z = (lax.dot_general(qbd, kt, nt, preferred_element_type=F32) if keys_major
             else jnp.dot(qbd, kt, preferred_element_type=F32)) + jnp.concatenate([bias] * n, axis=1)
        ls, lom = _stick_logs(z)
        if mask is not None:
            lom = jnp.where(mask, lom, 0.0)
        lom16 = lom.astype(BF16)
        lhs = jnp.concatenate([blk(lom16, p) for p in range(n)], axis=0)
        cs = jnp.dot(lhs, m2, preferred_element_type=F32)
        run = run_ref[...]
        args = []
        for p in range(n):
            csp = cs[p * rows:(p + 1) * rows]
            args.append(blk(ls, p) + csp[:, :KEY_BLOCK] + run)
            run = run + csp[:, KEY_BLOCK:]
        w = jnp.exp2(jnp.concatenate(args, axis=1))
        if mask is not None:
            w = jnp.where(mask, w, 0.0)
        acc_ref[...] += (jnp.dot(w, vt, preferred_element_type=F32) if keys_major
                         else lax.dot_general(w, vt, nt, preferred_element_type=F32))
        run_ref[...] = run

    def start():
        acc_ref[...] = jnp.zeros_like(acc_ref)
        run_ref[...] = jnp.zeros_like(run_ref)
        key = lax.broadcasted_iota(jnp.int32, (rows, KEY_BLOCK), 1)
        qi = lax.broadcasted_iota(jnp.int32, (rows, KEY_BLOCK), 0) // N_HEADS
        n_new = knew_ref.shape[1]
        place = (lax.broadcasted_iota(jnp.int32, (KEY_BLOCK, n_new), 0)
                 == lax.broadcasted_iota(jnp.int32, (KEY_BLOCK, n_new), 1)).astype(F32)
        k_blk = jnp.dot(place, knew_ref[0], preferred_element_type=F32)
        v_blk = jnp.dot(place, vnew_ref[0], preferred_element_type=F32)
        attend(k_blk, v_blk, key < qi, keys_major=True)

    def pages():
        attend(jnp.concatenate([r[...] for r in k_refs], axis=1),
               jnp.concatenate([r[...] for r in v_refs], axis=1), None)

    def finish():
        picked = jnp.where(head_sel, acc_ref[...], 0.0)
        out = [jnp.sum(picked[t * N_HEADS:(t + 1) * N_HEADS], axis=0, keepdims=True) for t in range(nq)]
        o_ref[0] = jnp.concatenate(out, axis=0)

    return start, pages, finish


def _pages_transposed(cache):
    _, n_pool, page, h, hd = cache.shape
    return jnp.transpose(cache, (0, 1, 3, 4, 2)).reshape(n_pool, h * hd, page)


def _step_conv_kernel(hist_ref, u_ref, w_ref, b_ref, g_ref, beta_ref, s_ref):
    n_hist = hist_ref.shape[0]
    for t in range(u_ref.shape[0]):
        acc = jnp.broadcast_to(b_ref[...], u_ref.shape[1:])
        for tap in range(CONV_W):
            r = t + tap + n_hist - (CONV_W - 1)
            rows = hist_ref[r] if r < n_hist else u_ref[r - n_hist]
            acc = acc + w_ref[tap:tap + 1, :] * rows
        y = _ln(acc) * g_ref[...] + beta_ref[...]
        s_ref[t] = _silu(y)


def _step_conv_module(u_t, hist_t, conv_w, conv_b, cln_g, cln_b):
    c = u_t.shape[-1]
    return pl.pallas_call(
        _step_conv_kernel,
        out_shape=jax.ShapeDtypeStruct(u_t.shape, F32),
        compiler_params=pltpu.CompilerParams(vmem_limit_bytes=VMEM_LIMIT),
        name="step_conv_module",
    )(hist_t, u_t, conv_w, conv_b.reshape(1, c), cln_g.reshape(1, c), cln_b.reshape(1, c))


def _post_phases(phase, x_ref, attn_ref, s_ref, gate1_ref, shift2_ref, scale2_ref, gate2_ref,
                 wo_ref, g1_ref, b1_ref, wgu_ref, wdown_ref, g2_ref, b2_ref, y_ref, x1_ref, h2_ref, act_ref,
                 alongside=None):
    c = attn_ref.shape[1]
    d_ff = wdown_ref.shape[0]
    fc = d_ff // 2

    def out_proj():
        mix = jnp.dot(attn_ref[...].astype(BF16), wo_ref[0:c, :], preferred_element_type=F32)
        mix = mix + jnp.dot(s_ref[...].astype(BF16), wo_ref[c:, :], preferred_element_type=F32)
        x1 = _ln(ALPHA * x_ref[...] + (1.0 + gate1_ref[...]) * mix) * g1_ref[...] + b1_ref[...]
        x1_ref[...] = x1
        h2_ref[...] = (_ln(x1) * (1.0 + scale2_ref[...]) + shift2_ref[...]).astype(BF16)

    def swiglu(j):
        h2 = h2_ref[...]
        gt = jnp.dot(h2, wgu_ref[:, j * fc:(j + 1) * fc], preferred_element_type=F32)
        up = jnp.dot(h2, wgu_ref[:, d_ff + j * fc:d_ff + (j + 1) * fc], preferred_element_type=F32)
        act_ref[:, j * fc:(j + 1) * fc] = (_silu(gt) * up).astype(BF16)

    def down_proj():
        f = jnp.dot(act_ref[...], wdown_ref[...], preferred_element_type=F32)
        y_ref[...] = _ln(ALPHA * x1_ref[...] + (1.0 + gate2_ref[...]) * f) * g2_ref[...] + b2_ref[...]

    parts = (out_proj, functools.partial(swiglu, 0), functools.partial(swiglu, 1), down_proj)
    assert len(parts) == POST_PHASES
    for k, part in enumerate(parts):
        if phase is None:
            part()
        else:
            @pl.when(phase == k)
            def _(part=part):
                part()
                if alongside is not None:
                    alongside()


def _post_kernel(*refs):
    _post_phases(None, *refs)


def _post_specs(mod, tm, rows_per_group, d, c, cc, d_ff, tile):
    tok = lambda w: pl.BlockSpec((tm, w), lambda i, *_: (tile(i), 0))
    in_specs = ([tok(d), tok(c), tok(cc)]
                + [_mod_spec(mod, tm, rows_per_group, ch, tile) for ch in (2, 3, 4, 5)]
                + [_const_spec((d, d)), _const_spec((1, d)), _const_spec((1, d)),
                   _const_spec((d, 2 * d_ff)), _const_spec((d_ff, d)), _const_spec((1, d)), _const_spec((1, d))])
    scratch = [pltpu.VMEM((tm, d), F32), pltpu.VMEM((tm, d), BF16), pltpu.VMEM((tm, d_ff), BF16)]
    return in_specs, tok(d), scratch


def _post(x, attn, s, mod, rows_per_group, w_o, ln1_g, ln1_b, w_gu, w_down, ln2_g, ln2_b, tm):
    n, d = x.shape
    d_ff = w_down.shape[0]
    vec = lambda a: a.reshape(1, d)
    in_specs, out_spec, scratch = _post_specs(mod, tm, rows_per_group, d, attn.shape[1], s.shape[1], d_ff, lambda i: i)
    return pl.pallas_call(
        _post_kernel,
        grid=(n // tm,),
        in_specs=in_specs,
        out_specs=out_spec,
        out_shape=jax.ShapeDtypeStruct((n, d), F32),
        scratch_shapes=scratch,
        compiler_params=pltpu.CompilerParams(dimension_semantics=("parallel",),
                                             vmem_limit_bytes=VMEM_LIMIT),
        name="post",
    )(x, attn, s, mod, mod, mod, mod, w_o, vec(ln1_g), vec(ln1_b), w_gu, w_down, vec(ln2_g), vec(ln2_b))


N_POST_INPUTS = 14


def _post_sample_kernel(pages_per_step, steps_per_seq, pt_ref, *refs):
    post_in = refs[:N_POST_INPUTS]
    q_ref, bias_ref, knew_ref, vnew_ref, m2_ref, kcache_ref, vcache_ref = refs[N_POST_INPUTS:N_POST_INPUTS + 7]
    y_ref, o_ref, x1_ref, h2_ref, act_ref, acc_ref, run_ref, kbuf, vbuf, sems = refs[N_POST_INPUTS + 7:]
    i = pl.program_id(0)
    n_steps = pl.num_programs(0)
    n_pages = steps_per_seq * pages_per_step
    step = i % steps_per_seq
    slot = i % PAGE_SLOTS

    def page_copies(j):
        seq = j // steps_per_seq
        latest = n_pages - 1 - (j % steps_per_seq) * pages_per_step
        dst = j % PAGE_SLOTS
        copies = []
        for p in range(pages_per_step):
            page = pt_ref[seq, latest - p]
            copies.append(pltpu.make_async_copy(kcache_ref.at[page], kbuf.at[dst, p], sems.at[0, dst]))
            copies.append(pltpu.make_async_copy(vcache_ref.at[page], vbuf.at[dst, p], sems.at[1, dst]))
        return copies

    ahead = PAGE_SLOTS - 1

    @pl.when(i == 0)
    def _():
        for j in range(ahead):
            for copy in page_copies(j):
                copy.start()

    @pl.when(i + ahead < n_steps)
    def _():
        for copy in page_copies(i + ahead):
            copy.start()

    for copy in page_copies(i):
        copy.wait()

    k_pages = [kbuf.at[slot, p] for p in range(pages_per_step)]
    v_pages = [vbuf.at[slot, p] for p in range(pages_per_step)]
    start, attend_pages, finish = _sample_attn_parts(q_ref, bias_ref, knew_ref, vnew_ref, m2_ref,
                                                     k_pages, v_pages, o_ref, acc_ref, run_ref)
    pl.when(step == 0)(start)
    _post_phases(i % POST_PHASES, *post_in, y_ref, x1_ref, h2_ref, act_ref, alongside=attend_pages)
    pl.when(step == steps_per_seq - 1)(finish)


def _post_and_sample_attn(x, attn, s, mod, rows_per_group, w_o, ln1_g, ln1_b, w_gu, w_down, ln2_g, ln2_b, tm,
                          q, k_new, v_new, cache_k, cache_v, page_table, sb_bias, m2, pages_per_step=16):
    n, d = x.shape
    d_ff = w_down.shape[0]
    b, nq, c = q.shape
    n_pages = page_table.shape[1]
    steps_per_seq = n_pages // pages_per_step
    n_steps = b * steps_per_seq
    assert n_steps == (n // tm) * POST_PHASES, "token tiles and page steps must pair up"
    rows = nq * N_HEADS
    bias = jnp.broadcast_to(jnp.tile(sb_bias, nq)[:, None], (rows, KEY_BLOCK)).astype(F32)
    pad = ((0, 0), (0, -nq % SUBLANES), (0, 0))
    k_new = jnp.pad(k_new, pad)
    v_new = jnp.pad(v_new, pad)
    n_new = k_new.shape[1]

    per_seq = lambda shape: pl.BlockSpec(shape, lambda i, pt: (i // steps_per_seq, 0, 0))
    post_specs, y_spec, post_scratch = _post_specs(mod, tm, rows_per_group, d, attn.shape[1], s.shape[1], d_ff,
                                                   lambda i: i // POST_PHASES)
    page_buf = pltpu.VMEM((PAGE_SLOTS, pages_per_step, c, KEY_BLOCK), cache_k.dtype)
    grid_spec = pltpu.PrefetchScalarGridSpec(
        num_scalar_prefetch=1,
        grid=(n_steps,),
        in_specs=post_specs
                 + [per_seq((1, nq, c)), pl.BlockSpec((rows, KEY_BLOCK), lambda i, pt: (0, 0)),
                    per_seq((1, n_new, c)), per_seq((1, n_new, c)),
                    pl.BlockSpec(m2.shape, lambda i, pt: (0, 0)),
                    pl.BlockSpec(memory_space=pl.ANY), pl.BlockSpec(memory_space=pl.ANY)],
        out_specs=[y_spec, per_seq((1, nq, c))],
        scratch_shapes=post_scratch + [pltpu.VMEM((rows, c), F32), pltpu.VMEM((rows, KEY_BLOCK), F32),
                                       page_buf, page_buf, pltpu.SemaphoreType.DMA((2, PAGE_SLOTS))],
    )
    vec = lambda a: a.reshape(1, d)
    return pl.pallas_call(
        functools.partial(_post_sample_kernel, pages_per_step, steps_per_seq),
        grid_spec=grid_spec,
        out_shape=[jax.ShapeDtypeStruct((n, d), F32), jax.ShapeDtypeStruct((b, nq, c), F32)],
        compiler_params=pltpu.CompilerParams(dimension_semantics=("arbitrary",), vmem_limit_bytes=VMEM_LIMIT),
        name="post_sample_attn",
    )(page_table, x, attn, s, mod, mod, mod, mod, w_o, vec(ln1_g), vec(ln1_b), w_gu, w_down, vec(ln2_g), vec(ln2_b),
      q, bias, k_new, v_new, m2, cache_k, cache_v)


def kernel(x_prompt, x_sample, c_prompt, c_sample, cache_k, cache_v, page_table, state_conv, w_ada, b_ada, w_in, sb_bias, conv_w, conv_b, cln_g, cln_b, w_o, ln1_g, ln1_b, w_gu, w_down, ln2_g, ln2_b):
    assert w_in.shape[0] == 1, "single-layer model"
    bp, tp, d = x_prompt.shape
    bs, ts, _ = x_sample.shape
    c = C_ATTN
    cc = state_conv.shape[-1]
    hist = CONV_W - 1

    w_in_b = w_in[0].astype(BF16)
    w_o_b = w_o[0].astype(BF16)
    w_gu_b = w_gu[0].astype(BF16)
    w_down_b = w_down[0].astype(BF16)
    m2 = _suffix_matrix()
    bias2 = sb_bias[0] * LOG2E

    mod = _ada(jnp.concatenate([c_prompt, c_sample], axis=0), w_ada[0], b_ada[0])
    mod_p = mod[:bp].reshape(bp, 1, 6 * d)
    mod_s = jnp.repeat(mod[bp:], ts, axis=0).reshape(1, bs * ts, 6 * d)

    xp = x_prompt.reshape(bp * tp, d)
    q_p, kt_p, vt_p, kb_p, vb_p, s_p, u_tail = _inproj_seq(xp, mod_p, tp, w_in_b, conv_w[0], conv_b[0],
                                                           cln_g[0], cln_b[0], tm=512)
    attn_p = _prompt_attn(q_p.reshape(bp, tp, c), kb_p.reshape(bp, tp, c), vb_p.reshape(bp, tp, c),
                          bias2, m2)

    xs = x_sample.reshape(bs * ts, d)
    q_s, k_s, v_s, u_s = _inproj_step(xs, mod_s, ts, w_in_b)
    y_p, attn_s = _post_and_sample_attn(
        xp, attn_p.reshape(bp * tp, c), s_p, mod_p, tp,
        w_o_b, ln1_g[0], ln1_b[0], w_gu_b, w_down_b, ln2_g[0], ln2_b[0], 256,
        q_s.reshape(bs, ts, c), k_s.reshape(bs, ts, c), v_s.reshape(bs, ts, c),
        _pages_transposed(cache_k), _pages_transposed(cache_v), page_table, bias2, m2)
    u_st = jnp.swapaxes(u_s.reshape(bs, ts, cc), 0, 1)
    hist_t = jnp.swapaxes(state_conv[0], 0, 1)
    s_s = jnp.swapaxes(_step_conv_module(u_st, hist_t, conv_w[0], conv_b[0], cln_g[0], cln_b[0]), 0, 1)
    y_s = _post(xs, attn_s.reshape(bs * ts, c), s_s.reshape(bs * ts, cc), mod_s, ts,
                w_o_b, ln1_g[0], ln1_b[0], w_gu_b, w_down_b, ln2_g[0], ln2_b[0], tm=bs * ts)

    heads = lambda a, b, t: a.reshape(1, b, t, N_HEADS, HEAD_DIM)
    heads_t = lambda a: jnp.transpose(a.reshape(1, bp, N_HEADS, HEAD_DIM, tp), (0, 1, 4, 2, 3))
    new_conv_p = u_tail[:, HALO - hist:][None]
    new_conv_s = jnp.swapaxes(jnp.concatenate([hist_t[ts:], u_st], axis=0), 0, 1)[None]
    return (y_p.reshape(bp, tp, d), y_s.reshape(bs, ts, d),
            heads_t(kt_p), heads_t(vt_p), new_conv_p,
            heads(k_s, bs, ts), heads(v_s, bs, ts), new_conv_s)
```

```python
import functools

import jax
import jax.numpy as jnp
from jax import lax
from jax.experimental import pallas as pl
from jax.experimental.pallas import tpu as pltpu

F32 = jnp.float32
BF16 = jnp.bfloat16

HEAD_DIM = 64
N_HEADS = 8
C_ATTN = N_HEADS * HEAD_DIM
CONV_W = 31
SUBLANES = 8
HALO = 32
CONV_ROWS = 512
LN_EPS = 1e-5
ALPHA = 2.0 ** 0.25
KEY_BLOCK = 128
LOG2E = 1.4426950408889634
Q_SCALE = HEAD_DIM ** -0.5 * LOG2E
MAX_NEG_LOGIT = 126.0
Q_TILE = 512
HEAD_PAIRS = 4
LOOP_BLOCKS = 4
POST_PHASES = 4
PAGE_SLOTS = 4
VMEM_LIMIT = 56 * 1024 * 1024
FUSED_VMEM_LIMIT = 62 * 1024 * 1024


def _ln(x):
    mu = jnp.mean(x, axis=-1, keepdims=True)
    xc = x - mu
    var = jnp.mean(xc * xc, axis=-1, keepdims=True)
    return xc * lax.rsqrt(var + LN_EPS)


def _silu(x):
    return x * jax.nn.sigmoid(x)


def _stick_logs(z2):
    ls = jnp.minimum(z2, 0.0) - jnp.log2(1.0 + jnp.exp2(-jnp.abs(z2)))
    return ls, ls - z2


def _stick_logs_neg(zn):
    zc = jnp.minimum(zn, MAX_NEG_LOGIT)
    return zc, zc - jnp.log2(1.0 + jnp.exp2(zc))


def _pair_suffix_matrix():
    r = lax.broadcasted_iota(jnp.int32, (2 * KEY_BLOCK, 2 * KEY_BLOCK), 0)
    c = lax.broadcasted_iota(jnp.int32, (2 * KEY_BLOCK, 2 * KEY_BLOCK), 1)
    return jnp.where((r // KEY_BLOCK == c // KEY_BLOCK) & (r >= c), 1.0, 0.0).astype(BF16)


def _suffix_matrix():
    r = lax.broadcasted_iota(jnp.int32, (KEY_BLOCK, 2 * KEY_BLOCK), 0)
    c = lax.broadcasted_iota(jnp.int32, (KEY_BLOCK, 2 * KEY_BLOCK), 1)
    return jnp.where((c >= KEY_BLOCK) | (r > c), 1.0, 0.0).astype(BF16)


def _ada_kernel(c_ref, w_ref, b_ref, o_ref):
    h = _silu(c_ref[...])
    o_ref[...] = jnp.dot(h, w_ref[...], preferred_element_type=F32) + b_ref[...]


def _ada(c, w_ada, b_ada):
    n, d = c.shape
    dn = w_ada.shape[1]
    tn = 1024
    return pl.pallas_call(
        _ada_kernel,
        grid=(dn // tn,),
        in_specs=[pl.BlockSpec((n, d), lambda j: (0, 0)),
                  pl.BlockSpec((d, tn), lambda j: (0, j)),
                  pl.BlockSpec((1, tn), lambda j: (0, j))],
        out_specs=pl.BlockSpec((n, tn), lambda j: (0, j)),
        out_shape=jax.ShapeDtypeStruct((n, dn), F32),
        compiler_params=pltpu.CompilerParams(dimension_semantics=("arbitrary",),
                                             vmem_limit_bytes=VMEM_LIMIT),
        name="ada",
    )(c, w_ada, b_ada.reshape(1, dn))


def _project(x_ref, shift_ref, scale_ref, w_ref, q_scale):
    c = C_ATTN
    cc = (w_ref.shape[1] - 3 * c) // 2
    h = (_ln(x_ref[...]) * (1.0 + scale_ref[...]) + shift_ref[...]).astype(BF16)
    dot = lambda lo, hi: jnp.dot(h, w_ref[:, lo:hi], preferred_element_type=F32)
    q = dot(0, c) * q_scale
    k = dot(c, 2 * c)
    v = dot(2 * c, 3 * c)
    u = dot(3 * c, 3 * c + cc) * jax.nn.sigmoid(dot(3 * c + cc, 3 * c + 2 * cc))
    return q, k, v, u


def _inproj_step_kernel(x_ref, shift_ref, scale_ref, w_ref, q_ref, k_ref, v_ref, u_ref):
    q, k, v, u = _project(x_ref, shift_ref, scale_ref, w_ref, Q_SCALE)
    q_ref[...] = q.astype(BF16)
    k_ref[...] = k
    v_ref[...] = v
    u_ref[...] = u


def _conv_rows(win_ref, phase_ref, w_ref, b_ref, g_ref, beta_ref, s_ref, tm):
    n = HALO + tm - SUBLANES
    for r in range(1, SUBLANES):
        phase_ref[r - 1, 0:n, :] = win_ref[r:r + n, :]

    def tap_rows(start):
        r = start % SUBLANES
        if r == 0:
            return win_ref[start:start + CONV_ROWS, :]
        return phase_ref[r - 1, start - r:start - r + CONV_ROWS, :]

    lead = HALO - (CONV_W - 1)
    for r0 in range(0, tm, CONV_ROWS):
        acc = jnp.broadcast_to(b_ref[...], (CONV_ROWS, b_ref.shape[1]))
        for tap in range(CONV_W):
            acc = acc + w_ref[tap:tap + 1, :] * tap_rows(r0 + lead + tap)
        y = _ln(acc) * g_ref[...] + beta_ref[...]
        s_ref[r0:r0 + CONV_ROWS, :] = _silu(y).astype(s_ref.dtype)


def _inproj_seq_kernel(tiles_per_seq, x_ref, shift_ref, scale_ref, w_ref, cw_ref, cb_ref, cg_ref, cbeta_ref,
                       q_ref, kt_ref, vt_ref, kb_ref, vb_ref, s_ref, tail_ref, win_ref, phase_ref):
    i = pl.program_id(0)
    tm = x_ref.shape[0]

    @pl.when(i == 0)
    def _():
        win_ref[...] = jnp.zeros_like(win_ref)

    q, k, v, u = _project(x_ref, shift_ref, scale_ref, w_ref, -Q_SCALE)
    q_ref[...] = q.astype(BF16)
    kb_ref[...] = k.astype(BF16)
    vb_ref[...] = v.astype(BF16)
    kt_ref[0] = k.T
    vt_ref[0] = v.T

    _conv_rows(win_ref, phase_ref, cw_ref, cb_ref, cg_ref, cbeta_ref, s_ref, tm)
    tail = win_ref[tm:tm + HALO, :]
    tail_ref[0] = tail
    win_ref[0:HALO, :] = jnp.where(i % tiles_per_seq == 0, 0.0, tail)
    win_ref[HALO:HALO + tm, :] = u


def _mod_spec(mod, tm, rows_per_group, chunk, tile=lambda i: i):
    d = mod.shape[-1] // 6
    if mod.shape[1] == 1:
        return pl.BlockSpec((None, 1, d), lambda i, *_: ((tile(i) * tm) // rows_per_group, 0, chunk))
    return pl.BlockSpec((None, tm, d), lambda i, *_: (0, tile(i), chunk))


def _const_spec(shape):
    return pl.BlockSpec(shape, lambda *_: (0,) * len(shape), pipeline_mode=pl.Buffered(1))


def _inproj_step(x, mod, rows_per_group, w_in):
    n, d = x.shape
    c = C_ATTN
    cc = (w_in.shape[1] - 3 * c) // 2
    tok = lambda w: pl.BlockSpec((n, w), lambda i: (0, 0))
    sds = jax.ShapeDtypeStruct
    return pl.pallas_call(
        _inproj_step_kernel,
        grid=(1,),
        in_specs=[tok(d), _mod_spec(mod, n, rows_per_group, 0), _mod_spec(mod, n, rows_per_group, 1),
                  _const_spec((d, w_in.shape[1]))],
        out_specs=[tok(c), tok(c), tok(c), tok(cc)],
        out_shape=[sds((n, c), BF16), sds((n, c), F32), sds((n, c), F32), sds((n, cc), F32)],
        compiler_params=pltpu.CompilerParams(dimension_semantics=("arbitrary",), vmem_limit_bytes=VMEM_LIMIT),
        name="inproj_step",
    )(x, mod, mod, w_in)


def _inproj_seq(x, mod, rows_per_group, w_in, conv_w, conv_b, cln_g, cln_b, tm):
    n, d = x.shape
    c = C_ATTN
    cc = (w_in.shape[1] - 3 * c) // 2
    assert rows_per_group % tm == 0 and tm % CONV_ROWS == 0 and tm >= HALO
    tiles = rows_per_group // tm
    n_tiles = n // tm
    n_seq = n // rows_per_group
    cur = lambda i: jnp.minimum(i, n_tiles - 1)
    prev = lambda i: jnp.maximum(i - 1, 0)
    tok = lambda w, tile: pl.BlockSpec((tm, w), lambda i: (tile(i), 0))
    kt_spec = pl.BlockSpec((1, c, tm), lambda i: (cur(i) // tiles, 0, cur(i) % tiles))
    vec = _const_spec((1, cc))
    sds = jax.ShapeDtypeStruct
    return pl.pallas_call(
        functools.partial(_inproj_seq_kernel, tiles),
        grid=(n_tiles + 1,),
        in_specs=[tok(d, cur), _mod_spec(mod, tm, rows_per_group, 0, cur), _mod_spec(mod, tm, rows_per_group, 1, cur),
                  _const_spec((d, w_in.shape[1])), _const_spec((CONV_W, cc)), vec, vec, vec],
        out_specs=[tok(c, cur), kt_spec, kt_spec, tok(c, cur), tok(c, cur), tok(cc, prev),
                   pl.BlockSpec((1, HALO, cc), lambda i: (prev(i) // tiles, 0, 0))],
        out_shape=[sds((n, c), BF16), sds((n_seq, c, rows_per_group), F32), sds((n_seq, c, rows_per_group), F32),
                   sds((n, c), BF16), sds((n, c), BF16), sds((n, cc), BF16), sds((n_seq, HALO, cc), F32)],
        scratch_shapes=[pltpu.VMEM((HALO + tm, cc), F32), pltpu.VMEM((SUBLANES - 1, HALO + tm, cc), F32)],
        compiler_params=pltpu.CompilerParams(dimension_semantics=("arbitrary",), vmem_limit_bytes=VMEM_LIMIT),
        name="inproj_seq",
    )(x, mod, mod, w_in, conv_w, conv_b.reshape(1, cc), cln_g.reshape(1, cc), cln_b.reshape(1, cc))


def _prompt_attn_kernel(bias_ref, q_ref, k_ref, v_ref, m2_ref, o_ref, acc_ref, run_ref):
    hg = pl.program_id(1)
    i = pl.program_id(2)
    tq = q_ref.shape[1]
    kb_per_tile = tq // KEY_BLOCK
    pw = 2 * HEAD_DIM
    m2 = m2_ref[...]
    lane = lax.broadcasted_iota(jnp.int32, (KEY_BLOCK, pw), 1)
    head_a = lane < HEAD_DIM
    col2 = lax.broadcasted_iota(jnp.int32, (1, 2 * KEY_BLOCK), 1)
    bias2 = [jnp.where(col2 < KEY_BLOCK, bias_ref[2 * (HEAD_PAIRS * hg + p)], bias_ref[2 * (HEAD_PAIRS * hg + p) + 1])
             for p in range(HEAD_PAIRS)]

    def block(j, r0, masked):
        start = pl.multiple_of(j * KEY_BLOCK, KEY_BLOCK)
        for p in range(HEAD_PAIRS):
            lanes = slice(p * pw, (p + 1) * pw)
            kb = k_ref[0, pl.ds(start, KEY_BLOCK), lanes]
            vb = v_ref[0, pl.ds(start, KEY_BLOCK), lanes]
            zero = jnp.zeros_like(kb)
            k2 = jnp.concatenate([jnp.where(head_a, kb, zero), jnp.where(head_a, zero, kb)], axis=0)
            v2 = jnp.concatenate([jnp.where(head_a, vb, zero), jnp.where(head_a, zero, vb)], axis=0)
            z = lax.dot_general(q_ref[0, r0:, lanes], k2, (((1,), (1,)), ((), ())),
                                preferred_element_type=F32) + bias2[p]
            zc, lom = _stick_logs_neg(z)
            if masked:
                n = tq - r0
                key = lax.broadcasted_iota(jnp.int32, (n, 2 * KEY_BLOCK), 1) % KEY_BLOCK
                qrow = lax.broadcasted_iota(jnp.int32, (n, 2 * KEY_BLOCK), 0)
                mask = key < qrow
                lom = jnp.where(mask, lom, 0.0)
            incl = jnp.dot(lom.astype(BF16), m2, preferred_element_type=F32)
            n_rows = incl.shape[0]
            total = jnp.concatenate([jnp.broadcast_to(incl[:, h * KEY_BLOCK:h * KEY_BLOCK + 1], (n_rows, KEY_BLOCK))
                                     for h in range(2)], axis=1)
            w = jnp.exp2(incl + run_ref[p, r0:, :] - zc)
            if masked:
                w = jnp.where(mask, w, 0.0)
            acc_ref[p, r0:, :] += jnp.dot(w.astype(BF16), v2, preferred_element_type=F32)
            run_ref[p, r0:, :] += total

    acc_ref[...] = jnp.zeros_like(acc_ref)
    run_ref[...] = jnp.zeros_like(run_ref)
    for jj in reversed(range(kb_per_tile)):
        block(i * kb_per_tile + jj, jj * KEY_BLOCK, True)

    def body(t, carry):
        for u in range(LOOP_BLOCKS):
            block(i * kb_per_tile - 1 - (t * LOOP_BLOCKS + u), 0, False)
        return carry

    lax.fori_loop(0, i * (kb_per_tile // LOOP_BLOCKS), body, 0)
    for p in range(HEAD_PAIRS):
        o_ref[0, :, p * pw:(p + 1) * pw] = acc_ref[p].astype(o_ref.dtype)


def _prompt_attn(q, k, v, neg_bias, m2):
    b, t, c = q.shape
    tq = Q_TILE
    hw = HEAD_PAIRS * 2 * HEAD_DIM
    assert t % tq == 0 and (tq // KEY_BLOCK) % LOOP_BLOCKS == 0 and c % hw == 0
    grid_spec = pltpu.PrefetchScalarGridSpec(
        num_scalar_prefetch=1,
        grid=(b, c // hw, t // tq),
        in_specs=[pl.BlockSpec((1, tq, hw), lambda bb, hg, i, bias: (bb, i, hg)),
                  pl.BlockSpec((1, t, hw), lambda bb, hg, i, bias: (bb, 0, hg)),
                  pl.BlockSpec((1, t, hw), lambda bb, hg, i, bias: (bb, 0, hg)),
                  pl.BlockSpec(m2.shape, lambda bb, hg, i, bias: (0, 0))],
        out_specs=pl.BlockSpec((1, tq, hw), lambda bb, hg, i, bias: (bb, i, hg)),
        scratch_shapes=[pltpu.VMEM((HEAD_PAIRS, tq, 2 * HEAD_DIM), F32),
                        pltpu.VMEM((HEAD_PAIRS, tq, 2 * KEY_BLOCK), F32)],
    )
    return pl.pallas_call(
        _prompt_attn_kernel,
        grid_spec=grid_spec,
        out_shape=jax.ShapeDtypeStruct((b, t, c), BF16),
        compiler_params=pltpu.CompilerParams(
            dimension_semantics=("parallel", "parallel", "arbitrary"), vmem_limit_bytes=VMEM_LIMIT),
        name="prompt_attn",
    )(neg_bias, q, k, v, m2)


def _sample_attn_parts(q_ref, bias_ref, knew_ref, vnew_ref, m2_ref, k_refs, v_refs, o_ref, acc_ref, run_ref):
    nq = q_ref.shape[1]
    rows = nq * N_HEADS
    c = q_ref.shape[2]
    row = lax.broadcasted_iota(jnp.int32, (rows, c), 0)
    colh = lax.broadcasted_iota(jnp.int32, (rows, c), 1) // HEAD_DIM
    q = q_ref[0].astype(F32)
    q_rows = jnp.concatenate([jnp.broadcast_to(q[t:t + 1], (N_HEADS, c)) for t in range(nq)], axis=0)
    head_sel = (row % N_HEADS) == colh
    qbd = jnp.where(head_sel, q_rows, 0.0)
    bias = bias_ref[...]
    m2 = m2_ref[...]

    nt = (((1,), (1,)), ((), ()))

    def attend(kt, vt, mask, keys_major=False):
        n = kt.shape[0 if keys_major else 1] // KEY_BLOCK
        blk = lambda a, p: a[:, p * KEY_BLOCK:(p + 1) * KEY_BLOCK]
        z = (lax.dot_general(qbd, kt, nt, preferred_element_type=F32) if keys_major
             else jnp.dot(qbd, kt, preferred_element_type=F32)) + jnp.concatenate([bias] * n, axis=1)
        ls, lom = _stick_logs(z)
        if mask is not None:
            lom = jnp.where(mask, lom, 0.0)
        lom16 = lom.astype(BF16)
        lhs = jnp.concatenate([blk(lom16, p) for p in range(n)], axis=0)
        cs = jnp.dot(lhs, m2, preferred_element_type=F32)
        run = run_ref[...]
        args = []
        for p in range(n):
            csp = cs[p * rows:(p + 1) * rows]
            args.append(blk(ls, p) + csp[:, :KEY_BLOCK] + run)
            run = run + csp[:, KEY_BLOCK:]
        w = jnp.exp2(jnp.concatenate(args, axis=1))
        if mask is not None:
            w = jnp.where(mask, w, 0.0)
        acc_ref[...] += (jnp.dot(w, vt, preferred_element_type=F32) if keys_major
                         else lax.dot_general(w, vt, nt, preferred_element_type=F32))
        run_ref[...] = run

    def start():
        acc_ref[...] = jnp.zeros_like(acc_ref)
        run_ref[...] = jnp.zeros_like(run_ref)
        key = lax.broadcasted_iota(jnp.int32, (rows, KEY_BLOCK), 1)
        qi = lax.broadcasted_iota(jnp.int32, (rows, KEY_BLOCK), 0) // N_HEADS
        n_new = knew_ref.shape[1]
        place = (lax.broadcasted_iota(jnp.int32, (KEY_BLOCK, n_new), 0)
                 == lax.broadcasted_iota(jnp.int32, (KEY_BLOCK, n_new), 1)).astype(F32)
        k_blk = jnp.dot(place, knew_ref[0], preferred_element_type=F32)
        v_blk = jnp.dot(place, vnew_ref[0], preferred_element_type=F32)
        attend(k_blk, v_blk, key < qi, keys_major=True)

    def pages():
        attend(jnp.concatenate([r[...] for r in k_refs], axis=1),
               jnp.concatenate([r[...] for r in v_refs], axis=1), None)

    def finish():
        picked = jnp.where(head_sel, acc_ref[...], 0.0)
        out = [jnp.sum(picked[t * N_HEADS:(t + 1) * N_HEADS], axis=0, keepdims=True) for t in range(nq)]
        o_ref[0] = jnp.concatenate(out, axis=0)

    return start, pages, finish


def _pages_transposed(cache):
    _, n_pool, page, h, hd = cache.shape
    return jnp.transpose(cache, (0, 1, 3, 4, 2)).reshape(n_pool, h * hd, page)


def _step_conv_kernel(hist_ref, u_ref, w_ref, b_ref, g_ref, beta_ref, s_ref):
    n_hist = hist_ref.shape[0]
    for t in range(u_ref.shape[0]):
        acc = jnp.broadcast_to(b_ref[...], u_ref.shape[1:])
        for tap in range(CONV_W):
            r = t + tap + n_hist - (CONV_W - 1)
            rows = hist_ref[r] if r < n_hist else u_ref[r - n_hist]
            acc = acc + w_ref[tap:tap + 1, :] * rows
        y = _ln(acc) * g_ref[...] + beta_ref[...]
        s_ref[t] = _silu(y)


def _step_conv_module(u_t, hist_t, conv_w, conv_b, cln_g, cln_b):
    c = u_t.shape[-1]
    return pl.pallas_call(
        _step_conv_kernel,
        out_shape=jax.ShapeDtypeStruct(u_t.shape, F32),
        compiler_params=pltpu.CompilerParams(vmem_limit_bytes=VMEM_LIMIT),
        name="step_conv_module",
    )(hist_t, u_t, conv_w, conv_b.reshape(1, c), cln_g.reshape(1, c), cln_b.reshape(1, c))


def _post_phases(phase, x_ref, attn_ref, s_ref, gate1_ref, shift2_ref, scale2_ref, gate2_ref,
                 wo_ref, g1_ref, b1_ref, wgu_ref, wdown_ref, g2_ref, b2_ref, y_ref, x1_ref, h2_ref, act_ref,
                 alongside=None):
    c = attn_ref.shape[1]
    d_ff = wdown_ref.shape[0]
    fc = d_ff // 2

    def out_proj():
        mix = jnp.dot(attn_ref[...].astype(BF16), wo_ref[0:c, :], preferred_element_type=F32)
        mix = mix + jnp.dot(s_ref[...].astype(BF16), wo_ref[c:, :], preferred_element_type=F32)
        x1 = _ln(ALPHA * x_ref[...] + (1.0 + gate1_ref[...]) * mix) * g1_ref[...] + b1_ref[...]
        x1_ref[...] = x1
        h2_ref[...] = (_ln(x1) * (1.0 + scale2_ref[...]) + shift2_ref[...]).astype(BF16)

    def swiglu(j):
        h2 = h2_ref[...]
        gt = jnp.dot(h2, wgu_ref[:, j * fc:(j + 1) * fc], preferred_element_type=F32)
        up = jnp.dot(h2, wgu_ref[:, d_ff + j * fc:d_ff + (j + 1) * fc], preferred_element_type=F32)
        act_ref[:, j * fc:(j + 1) * fc] = (_silu(gt) * up).astype(BF16)

    def down_proj():
        f = jnp.dot(act_ref[...], wdown_ref[...], preferred_element_type=F32)
        y_ref[...] = _ln(ALPHA * x1_ref[...] + (1.0 + gate2_ref[...]) * f) * g2_ref[...] + b2_ref[...]

    parts = (out_proj, functools.partial(swiglu, 0), functools.partial(swiglu, 1), down_proj)
    assert len(parts) == POST_PHASES
    for k, part in enumerate(parts):
        if phase is None:
            part()
        else:
            @pl.when(phase == k)
            def _(part=part):
                part()
                if alongside is not None:
                    alongside()


def _post_kernel(*refs):
    _post_phases(None, *refs)


def _post_specs(mod, tm, rows_per_group, d, c, cc, d_ff, tile):
    tok = lambda w: pl.BlockSpec((tm, w), lambda i, *_: (tile(i), 0))
    in_specs = ([tok(d), tok(c), tok(cc)]
                + [_mod_spec(mod, tm, rows_per_group, ch, tile) for ch in (2, 3, 4, 5)]
                + [_const_spec((d, d)), _const_spec((1, d)), _const_spec((1, d)),
                   _const_spec((d, 2 * d_ff)), _const_spec((d_ff, d)), _const_spec((1, d)), _const_spec((1, d))])
    scratch = [pltpu.VMEM((tm, d), F32), pltpu.VMEM((tm, d), BF16), pltpu.VMEM((tm, d_ff), BF16)]
    return in_specs, tok(d), scratch


def _post(x, attn, s, mod, rows_per_group, w_o, ln1_g, ln1_b, w_gu, w_down, ln2_g, ln2_b, tm):
    n, d = x.shape
    d_ff = w_down.shape[0]
    vec = lambda a: a.reshape(1, d)
    in_specs, out_spec, scratch = _post_specs(mod, tm, rows_per_group, d, attn.shape[1], s.shape[1], d_ff, lambda i: i)
    return pl.pallas_call(
        _post_kernel,
        grid=(n // tm,),
        in_specs=in_specs,
        out_specs=out_spec,
        out_shape=jax.ShapeDtypeStruct((n, d), F32),
        scratch_shapes=scratch,
        compiler_params=pltpu.CompilerParams(dimension_semantics=("parallel",),
                                             vmem_limit_bytes=VMEM_LIMIT),
        name="post",
    )(x, attn, s, mod, mod, mod, mod, w_o, vec(ln1_g), vec(ln1_b), w_gu, w_down, vec(ln2_g), vec(ln2_b))


N_POST_INPUTS = 14


def _post_sample_kernel(pages_per_step, steps_per_seq, pt_ref, *refs):
    post_in = refs[:N_POST_INPUTS]
    q_ref, bias_ref, knew_ref, vnew_ref, m2_ref, kcache_ref, vcache_ref = refs[N_POST_INPUTS:N_POST_INPUTS + 7]
    y_ref, o_ref, x1_ref, h2_ref, act_ref, acc_ref, run_ref, kbuf, vbuf, sems = refs[N_POST_INPUTS + 7:]
    i = pl.program_id(0)
    n_steps = pl.num_programs(0)
    n_pages = steps_per_seq * pages_per_step
    step = i % steps_per_seq
    slot = i % PAGE_SLOTS

    def page_copies(j):
        seq = j // steps_per_seq
        latest = n_pages - 1 - (j % steps_per_seq) * pages_per_step
        dst = j % PAGE_SLOTS
        copies = []
        for p in range(pages_per_step):
            page = pt_ref[seq, latest - p]
            copies.append(pltpu.make_async_copy(kcache_ref.at[page], kbuf.at[dst, p], sems.at[0, dst]))
            copies.append(pltpu.make_async_copy(vcache_ref.at[page], vbuf.at[dst, p], sems.at[1, dst]))
        return copies

    ahead = PAGE_SLOTS - 1

    @pl.when(i == 0)
    def _():
        for j in range(ahead):
            for copy in page_copies(j):
                copy.start()

    @pl.when(i + ahead < n_steps)
    def _():
        for copy in page_copies(i + ahead):
            copy.start()

    for copy in page_copies(i):
        copy.wait()

    k_pages = [kbuf.at[slot, p] for p in range(pages_per_step)]
    v_pages = [vbuf.at[slot, p] for p in range(pages_per_step)]
    start, attend_pages, finish = _sample_attn_parts(q_ref, bias_ref, knew_ref, vnew_ref, m2_ref,
                                                     k_pages, v_pages, o_ref, acc_ref, run_ref)
    pl.when(step == 0)(start)
    _post_phases(i % POST_PHASES, *post_in, y_ref, x1_ref, h2_ref, act_ref, alongside=attend_pages)
    pl.when(step == steps_per_seq - 1)(finish)


def _post_and_sample_attn(x, attn, s, mod, rows_per_group, w_o, ln1_g, ln1_b, w_gu, w_down, ln2_g, ln2_b, tm,
                          q, k_new, v_new, cache_k, cache_v, page_table, sb_bias, m2, pages_per_step=16):
    n, d = x.shape
    d_ff = w_down.shape[0]
    b, nq, c = q.shape
    n_pages = page_table.shape[1]
    steps_per_seq = n_pages // pages_per_step
    n_steps = b * steps_per_seq
    assert n_steps == (n // tm) * POST_PHASES, "token tiles and page steps must pair up"
    rows = nq * N_HEADS
    bias = jnp.broadcast_to(jnp.tile(sb_bias, nq)[:, None], (rows, KEY_BLOCK)).astype(F32)
    pad = ((0, 0), (0, -nq % SUBLANES), (0, 0))
    k_new = jnp.pad(k_new, pad)
    v_new = jnp.pad(v_new, pad)
    n_new = k_new.shape[1]

    per_seq = lambda shape: pl.BlockSpec(shape, lambda i, pt: (i // steps_per_seq, 0, 0))
    post_specs, y_spec, post_scratch = _post_specs(mod, tm, rows_per_group, d, attn.shape[1], s.shape[1], d_ff,
                                                   lambda i: i // POST_PHASES)
    page_buf = pltpu.VMEM((PAGE_SLOTS, pages_per_step, c, KEY_BLOCK), cache_k.dtype)
    grid_spec = pltpu.PrefetchScalarGridSpec(
        num_scalar_prefetch=1,
        grid=(n_steps,),
        in_specs=post_specs
                 + [per_seq((1, nq, c)), pl.BlockSpec((rows, KEY_BLOCK), lambda i, pt: (0, 0)),
                    per_seq((1, n_new, c)), per_seq((1, n_new, c)),
                    pl.BlockSpec(m2.shape, lambda i, pt: (0, 0)),
                    pl.BlockSpec(memory_space=pl.ANY), pl.BlockSpec(memory_space=pl.ANY)],
        out_specs=[y_spec, per_seq((1, nq, c))],
        scratch_shapes=post_scratch + [pltpu.VMEM((rows, c), F32), pltpu.VMEM((rows, KEY_BLOCK), F32),
                                       page_buf, page_buf, pltpu.SemaphoreType.DMA((2, PAGE_SLOTS))],
    )
    vec = lambda a: a.reshape(1, d)
    return pl.pallas_call(
        functools.partial(_post_sample_kernel, pages_per_step, steps_per_seq),
        grid_spec=grid_spec,
        out_shape=[jax.ShapeDtypeStruct((n, d), F32), jax.ShapeDtypeStruct((b, nq, c), F32)],
        compiler_params=pltpu.CompilerParams(dimension_semantics=("arbitrary",), vmem_limit_bytes=FUSED_VMEM_LIMIT),
        name="post_sample_attn",
    )(page_table, x, attn, s, mod, mod, mod, mod, w_o, vec(ln1_g), vec(ln1_b), w_gu, w_down, vec(ln2_g), vec(ln2_b),
      q, bias, k_new, v_new, m2, cache_k, cache_v)


def kernel(x_prompt, x_sample, c_prompt, c_sample, cache_k, cache_v, page_table, state_conv, w_ada, b_ada, w_in, sb_bias, conv_w, conv_b, cln_g, cln_b, w_o, ln1_g, ln1_b, w_gu, w_down, ln2_g, ln2_b):
    assert w_in.shape[0] == 1, "single-layer model"
    bp, tp, d = x_prompt.shape
    bs, ts, _ = x_sample.shape
    c = C_ATTN
    cc = state_conv.shape[-1]
    hist = CONV_W - 1

    w_in_b = w_in[0].astype(BF16)
    w_o_b = w_o[0].astype(BF16)
    w_gu_b = w_gu[0].astype(BF16)
    w_down_b = w_down[0].astype(BF16)
    m2 = _suffix_matrix()
    bias2 = sb_bias[0] * LOG2E

    mod = _ada(jnp.concatenate([c_prompt, c_sample], axis=0), w_ada[0], b_ada[0])
    mod_p = mod[:bp].reshape(bp, 1, 6 * d)
    mod_s = jnp.repeat(mod[bp:], ts, axis=0).reshape(1, bs * ts, 6 * d)

    xp = x_prompt.reshape(bp * tp, d)
    q_p, kt_p, vt_p, kb_p, vb_p, s_p, u_tail = _inproj_seq(xp, mod_p, tp, w_in_b, conv_w[0], conv_b[0],
                                                           cln_g[0], cln_b[0], tm=512)
    attn_p = _prompt_attn(q_p.reshape(bp, tp, c), kb_p.reshape(bp, tp, c), vb_p.reshape(bp, tp, c),
                          -bias2, _pair_suffix_matrix())

    xs = x_sample.reshape(bs * ts, d)
    q_s, k_s, v_s, u_s = _inproj_step(xs, mod_s, ts, w_in_b)
    y_p, attn_s = _post_and_sample_attn(
        xp, attn_p.reshape(bp * tp, c), s_p, mod_p, tp,
        w_o_b, ln1_g[0], ln1_b[0], w_gu_b, w_down_b, ln2_g[0], ln2_b[0], 256,
        q_s.reshape(bs, ts, c), k_s.reshape(bs, ts, c), v_s.reshape(bs, ts, c),
        _pages_transposed(cache_k), _pages_transposed(cache_v), page_table, bias2, m2)
    u_st = jnp.swapaxes(u_s.reshape(bs, ts, cc), 0, 1)
    hist_t = jnp.swapaxes(state_conv[0], 0, 1)
    s_s = jnp.swapaxes(_step_conv_module(u_st, hist_t, conv_w[0], conv_b[0], cln_g[0], cln_b[0]), 0, 1)
    y_s = _post(xs, attn_s.reshape(bs * ts, c), s_s.reshape(bs * ts, cc), mod_s, ts,
                w_o_b, ln1_g[0], ln1_b[0], w_gu_b, w_down_b, ln2_g[0], ln2_b[0], tm=bs * ts)

    heads = lambda a, b, t: a.reshape(1, b, t, N_HEADS, HEAD_DIM)
    heads_t = lambda a: jnp.transpose(a.reshape(1, bp, N_HEADS, HEAD_DIM, tp), (0, 1, 4, 2, 3))
    new_conv_p = u_tail[:, HALO - hist:][None]
    new_conv_s = jnp.swapaxes(jnp.concatenate([hist_t[ts:], u_st], axis=0), 0, 1)[None]
    return (y_p.reshape(bp, tp, d), y_s.reshape(bs, ts, d),
            heads_t(kt_p), heads_t(vt_p), new_conv_p,
            heads(k_s, bs, ts), heads(v_s, bs, ts), new_conv_s)
```

```python
import functools

import jax
import jax.numpy as jnp
from jax import lax
from jax.experimental import pallas as pl
from jax.experimental.pallas import tpu as pltpu

F32 = jnp.float32
BF16 = jnp.bfloat16

HEAD_DIM = 64
N_HEADS = 8
C_ATTN = N_HEADS * HEAD_DIM
CONV_W = 31
SUBLANES = 8
HALO = 32
CONV_ROWS = 512
LN_EPS = 1e-5
ALPHA = 2.0 ** 0.25
KEY_BLOCK = 128
LOG2E = 1.4426950408889634
Q_SCALE = HEAD_DIM ** -0.5 * LOG2E
MAX_NEG_LOGIT = 126.0
Q_TILE = 512
HEAD_PAIRS = 4
LOOP_BLOCKS = 4
POST_PHASES = 4
PAGE_SLOTS = 3
VMEM_LIMIT = 56 * 1024 * 1024


def _ln(x):
    mu = jnp.mean(x, axis=-1, keepdims=True)
    xc = x - mu
    var = jnp.mean(xc * xc, axis=-1, keepdims=True)
    return xc * lax.rsqrt(var + LN_EPS)


def _silu(x):
    return x * jax.nn.sigmoid(x)


def _stick_logs(z2):
    ls = jnp.minimum(z2, 0.0) - jnp.log2(1.0 + jnp.exp2(-jnp.abs(z2)))
    return ls, ls - z2


def _stick_logs_neg(zn):
    zc = jnp.minimum(zn, MAX_NEG_LOGIT)
    return zc, zc - jnp.log2(1.0 + jnp.exp2(zc))


def _pair_suffix_matrix():
    r = lax.broadcasted_iota(jnp.int32, (2 * KEY_BLOCK, 2 * KEY_BLOCK), 0)
    c = lax.broadcasted_iota(jnp.int32, (2 * KEY_BLOCK, 2 * KEY_BLOCK), 1)
    return jnp.where((r // KEY_BLOCK == c // KEY_BLOCK) & (r >= c), 1.0, 0.0).astype(BF16)


def _suffix_matrix():
    r = lax.broadcasted_iota(jnp.int32, (KEY_BLOCK, 2 * KEY_BLOCK), 0)
    c = lax.broadcasted_iota(jnp.int32, (KEY_BLOCK, 2 * KEY_BLOCK), 1)
    return jnp.where((c >= KEY_BLOCK) | (r > c), 1.0, 0.0).astype(BF16)


def _ada_kernel(c_ref, w_ref, b_ref, o_ref):
    h = _silu(c_ref[...])
    o_ref[...] = jnp.dot(h, w_ref[...], preferred_element_type=F32) + b_ref[...]


def _ada(c, w_ada, b_ada):
    n, d = c.shape
    dn = w_ada.shape[1]
    tn = 2048
    return pl.pallas_call(
        _ada_kernel,
        grid=(dn // tn,),
        in_specs=[pl.BlockSpec((n, d), lambda j: (0, 0)),
                  pl.BlockSpec((d, tn), lambda j: (0, j)),
                  pl.BlockSpec((1, tn), lambda j: (0, j))],
        out_specs=pl.BlockSpec((n, tn), lambda j: (0, j)),
        out_shape=jax.ShapeDtypeStruct((n, dn), F32),
        compiler_params=pltpu.CompilerParams(dimension_semantics=("arbitrary",),
                                             vmem_limit_bytes=VMEM_LIMIT),
        name="ada",
    )(c, w_ada, b_ada.reshape(1, dn))


def _project(x_ref, shift_ref, scale_ref, w_ref, q_scale):
    c = C_ATTN
    cc = (w_ref.shape[1] - 3 * c) // 2
    h = (_ln(x_ref[...]) * (1.0 + scale_ref[...]) + shift_ref[...]).astype(BF16)
    dot = lambda lo, hi: jnp.dot(h, w_ref[:, lo:hi], preferred_element_type=F32)
    q = dot(0, c) * q_scale
    k = dot(c, 2 * c)
    v = dot(2 * c, 3 * c)
    u = dot(3 * c, 3 * c + cc) * jax.nn.sigmoid(dot(3 * c + cc, 3 * c + 2 * cc))
    return q, k, v, u


def _inproj_step_kernel(x_ref, shift_ref, scale_ref, w_ref, q_ref, k_ref, v_ref, u_ref):
    q, k, v, u = _project(x_ref, shift_ref, scale_ref, w_ref, Q_SCALE)
    q_ref[...] = q.astype(BF16)
    k_ref[...] = k
    v_ref[...] = v
    u_ref[...] = u


def _conv_rows(win_ref, phase_ref, w_ref, b_ref, g_ref, beta_ref, s_ref, tm):
    n = HALO + tm - SUBLANES
    for r in range(1, SUBLANES):
        phase_ref[r - 1, 0:n, :] = win_ref[r:r + n, :]

    def tap_rows(start):
        r = start % SUBLANES
        if r == 0:
            return win_ref[start:start + CONV_ROWS, :]
        return phase_ref[r - 1, start - r:start - r + CONV_ROWS, :]

    lead = HALO - (CONV_W - 1)
    for r0 in range(0, tm, CONV_ROWS):
        acc = jnp.broadcast_to(b_ref[...], (CONV_ROWS, b_ref.shape[1]))
        for tap in range(CONV_W):
            acc = acc + w_ref[tap:tap + 1, :] * tap_rows(r0 + lead + tap)
        y = _ln(acc) * g_ref[...] + beta_ref[...]
        s_ref[r0:r0 + CONV_ROWS, :] = _silu(y).astype(s_ref.dtype)


def _inproj_seq_kernel(tiles_per_seq, x_ref, shift_ref, scale_ref, w_ref, cw_ref, cb_ref, cg_ref, cbeta_ref,
                       q_ref, kt_ref, vt_ref, kb_ref, vb_ref, s_ref, tail_ref, win_ref, phase_ref):
    i = pl.program_id(0)
    tm = x_ref.shape[0]

    @pl.when(i == 0)
    def _():
        win_ref[...] = jnp.zeros_like(win_ref)

    q, k, v, u = _project(x_ref, shift_ref, scale_ref, w_ref, -Q_SCALE)
    q_ref[...] = q.astype(BF16)
    kb_ref[...] = k.astype(BF16)
    vb_ref[...] = v.astype(BF16)
    kt_ref[0] = k.T
    vt_ref[0] = v.T

    _conv_rows(win_ref, phase_ref, cw_ref, cb_ref, cg_ref, cbeta_ref, s_ref, tm)
    tail = win_ref[tm:tm + HALO, :]
    tail_ref[0] = tail
    win_ref[0:HALO, :] = jnp.where(i % tiles_per_seq == 0, 0.0, tail)
    win_ref[HALO:HALO + tm, :] = u


def _mod_spec(mod, tm, rows_per_group, chunk, tile=lambda i: i):
    d = mod.shape[-1] // 6
    if mod.shape[1] == 1:
        return pl.BlockSpec((None, 1, d), lambda i, *_: ((tile(i) * tm) // rows_per_group, 0, chunk))
    return pl.BlockSpec((None, tm, d), lambda i, *_: (0, tile(i), chunk))


def _const_spec(shape):
    return pl.BlockSpec(shape, lambda *_: (0,) * len(shape), pipeline_mode=pl.Buffered(1))


def _inproj_step(x, mod, rows_per_group, w_in):
    n, d = x.shape
    c = C_ATTN
    cc = (w_in.shape[1] - 3 * c) // 2
    tok = lambda w: pl.BlockSpec((n, w), lambda i: (0, 0))
    sds = jax.ShapeDtypeStruct
    return pl.pallas_call(
        _inproj_step_kernel,
        grid=(1,),
        in_specs=[tok(d), _mod_spec(mod, n, rows_per_group, 0), _mod_spec(mod, n, rows_per_group, 1),
                  _const_spec((d, w_in.shape[1]))],
        out_specs=[tok(c), tok(c), tok(c), tok(cc)],
        out_shape=[sds((n, c), BF16), sds((n, c), F32), sds((n, c), F32), sds((n, cc), F32)],
        compiler_params=pltpu.CompilerParams(dimension_semantics=("arbitrary",), vmem_limit_bytes=VMEM_LIMIT),
        name="inproj_step",
    )(x, mod, mod, w_in)


def _inproj_seq(x, mod, rows_per_group, w_in, conv_w, conv_b, cln_g, cln_b, tm):
    n, d = x.shape
    c = C_ATTN
    cc = (w_in.shape[1] - 3 * c) // 2
    assert rows_per_group % tm == 0 and tm % CONV_ROWS == 0 and tm >= HALO
    tiles = rows_per_group // tm
    n_tiles = n // tm
    n_seq = n // rows_per_group
    cur = lambda i: jnp.minimum(i, n_tiles - 1)
    prev = lambda i: jnp.maximum(i - 1, 0)
    tok = lambda w, tile: pl.BlockSpec((tm, w), lambda i: (tile(i), 0))
    kt_spec = pl.BlockSpec((1, c, tm), lambda i: (cur(i) // tiles, 0, cur(i) % tiles))
    vec = _const_spec((1, cc))
    sds = jax.ShapeDtypeStruct
    return pl.pallas_call(
        functools.partial(_inproj_seq_kernel, tiles),
        grid=(n_tiles + 1,),
        in_specs=[tok(d, cur), _mod_spec(mod, tm, rows_per_group, 0, cur), _mod_spec(mod, tm, rows_per_group, 1, cur),
                  _const_spec((d, w_in.shape[1])), _const_spec((CONV_W, cc)), vec, vec, vec],
        out_specs=[tok(c, cur), kt_spec, kt_spec, tok(c, cur), tok(c, cur), tok(cc, prev),
                   pl.BlockSpec((1, HALO, cc), lambda i: (prev(i) // tiles, 0, 0))],
        out_shape=[sds((n, c), BF16), sds((n_seq, c, rows_per_group), F32), sds((n_seq, c, rows_per_group), F32),
                   sds((n, c), BF16), sds((n, c), BF16), sds((n, cc), BF16), sds((n_seq, HALO, cc), F32)],
        scratch_shapes=[pltpu.VMEM((HALO + tm, cc), F32), pltpu.VMEM((SUBLANES - 1, HALO + tm, cc), F32)],
        compiler_params=pltpu.CompilerParams(dimension_semantics=("arbitrary",), vmem_limit_bytes=VMEM_LIMIT),
        name="inproj_seq",
    )(x, mod, mod, w_in, conv_w, conv_b.reshape(1, cc), cln_g.reshape(1, cc), cln_b.reshape(1, cc))


def _prompt_attn_kernel(bias_ref, q_ref, k_ref, v_ref, m2_ref, o_ref, acc_ref, run_ref):
    hg = pl.program_id(1)
    i = pl.program_id(2)
    tq = q_ref.shape[1]
    kb_per_tile = tq // KEY_BLOCK
    pw = 2 * HEAD_DIM
    m2 = m2_ref[...]
    lane = lax.broadcasted_iota(jnp.int32, (KEY_BLOCK, pw), 1)
    head_a = lane < HEAD_DIM
    col2 = lax.broadcasted_iota(jnp.int32, (1, 2 * KEY_BLOCK), 1)
    bias2 = [jnp.where(col2 < KEY_BLOCK, bias_ref[2 * (HEAD_PAIRS * hg + p)], bias_ref[2 * (HEAD_PAIRS * hg + p) + 1])
             for p in range(HEAD_PAIRS)]

    def block(j, r0, masked):
        start = pl.multiple_of(j * KEY_BLOCK, KEY_BLOCK)
        for p in range(HEAD_PAIRS):
            lanes = slice(p * pw, (p + 1) * pw)
            kb = k_ref[0, pl.ds(start, KEY_BLOCK), lanes]
            vb = v_ref[0, pl.ds(start, KEY_BLOCK), lanes]
            zero = jnp.zeros_like(kb)
            k2 = jnp.concatenate([jnp.where(head_a, kb, zero), jnp.where(head_a, zero, kb)], axis=0)
            v2 = jnp.concatenate([jnp.where(head_a, vb, zero), jnp.where(head_a, zero, vb)], axis=0)
            z = lax.dot_general(q_ref[0, r0:, lanes], k2, (((1,), (1,)), ((), ())),
                                preferred_element_type=F32) + bias2[p]
            zc, lom = _stick_logs_neg(z)
            if masked:
                n = tq - r0
                key = lax.broadcasted_iota(jnp.int32, (n, 2 * KEY_BLOCK), 1) % KEY_BLOCK
                qrow = lax.broadcasted_iota(jnp.int32, (n, 2 * KEY_BLOCK), 0)
                mask = key < qrow
                lom = jnp.where(mask, lom, 0.0)
            incl = jnp.dot(lom.astype(BF16), m2, preferred_element_type=F32)
            n_rows = incl.shape[0]
            total = jnp.concatenate([jnp.broadcast_to(incl[:, h * KEY_BLOCK:h * KEY_BLOCK + 1], (n_rows, KEY_BLOCK))
                                     for h in range(2)], axis=1)
            w = jnp.exp2(incl + run_ref[p, r0:, :] - zc)
            if masked:
                w = jnp.where(mask, w, 0.0)
            acc_ref[p, r0:, :] += jnp.dot(w.astype(BF16), v2, preferred_element_type=F32)
            run_ref[p, r0:, :] += total

    acc_ref[...] = jnp.zeros_like(acc_ref)
    run_ref[...] = jnp.zeros_like(run_ref)
    for jj in reversed(range(kb_per_tile)):
        block(i * kb_per_tile + jj, jj * KEY_BLOCK, True)

    def body(t, carry):
        for u in range(LOOP_BLOCKS):
            block(i * kb_per_tile - 1 - (t * LOOP_BLOCKS + u), 0, False)
        return carry

    lax.fori_loop(0, i * (kb_per_tile // LOOP_BLOCKS), body, 0)
    for p in range(HEAD_PAIRS):
        o_ref[0, :, p * pw:(p + 1) * pw] = acc_ref[p].astype(o_ref.dtype)


def _prompt_attn(q, k, v, neg_bias, m2):
    b, t, c = q.shape
    tq = Q_TILE
    hw = HEAD_PAIRS * 2 * HEAD_DIM
    assert t % tq == 0 and (tq // KEY_BLOCK) % LOOP_BLOCKS == 0 and c % hw == 0
    grid_spec = pltpu.PrefetchScalarGridSpec(
        num_scalar_prefetch=1,
        grid=(b, c // hw, t // tq),
        in_specs=[pl.BlockSpec((1, tq, hw), lambda bb, hg, i, bias: (bb, i, hg)),
                  pl.BlockSpec((1, t, hw), lambda bb, hg, i, bias: (bb, 0, hg)),
                  pl.BlockSpec((1, t, hw), lambda bb, hg, i, bias: (bb, 0, hg)),
                  pl.BlockSpec(m2.shape, lambda bb, hg, i, bias: (0, 0))],
        out_specs=pl.BlockSpec((1, tq, hw), lambda bb, hg, i, bias: (bb, i, hg)),
        scratch_shapes=[pltpu.VMEM((HEAD_PAIRS, tq, 2 * HEAD_DIM), F32),
                        pltpu.VMEM((HEAD_PAIRS, tq, 2 * KEY_BLOCK), F32)],
    )
    return pl.pallas_call(
        _prompt_attn_kernel,
        grid_spec=grid_spec,
        out_shape=jax.ShapeDtypeStruct((b, t, c), BF16),
        compiler_params=pltpu.CompilerParams(
            dimension_semantics=("parallel", "parallel", "arbitrary"), vmem_limit_bytes=VMEM_LIMIT),
        name="prompt_attn",
    )(neg_bias, q, k, v, m2)


def _sample_attn_parts(q_ref, bias_ref, knew_ref, vnew_ref, m2_ref, k_refs, v_refs, o_ref, acc_ref, run_ref):
    nq = q_ref.shape[1]
    rows = nq * N_HEADS
    c = q_ref.shape[2]
    row = lax.broadcasted_iota(jnp.int32, (rows, c), 0)
    colh = lax.broadcasted_iota(jnp.int32, (rows, c), 1) // HEAD_DIM
    q = q_ref[0].astype(F32)
    q_rows = jnp.concatenate([jnp.broadcast_to(q[t:t + 1], (N_HEADS, c)) for t in range(nq)], axis=0)
    head_sel = (row % N_HEADS) == colh
    qbd = jnp.where(head_sel, q_rows, 0.0)
    bias = bias_ref[...]
    m2 = m2_ref[...]

    nt = (((1,), (1,)), ((), ()))

    def attend(kt, vt, mask, keys_major=False):
        n = kt.shape[0 if keys_major else 1] // KEY_BLOCK
        blk = lambda a, p: a[:, p * KEY_BLOCK:(p + 1) * KEY_BLOCK]
        z = (lax.dot_general(qbd, kt, nt, preferred_element_type=F32) if keys_major
             else jnp.dot(qbd, kt, preferred_element_type=F32)) + jnp.concatenate([bias] * n, axis=1)
        ls, lom = _stick_logs(z)
        if mask is not None:
            lom = jnp.where(mask, lom, 0.0)
        lom16 = lom.astype(BF16)
        lhs = jnp.concatenate([blk(lom16, p) for p in range(n)], axis=0)
        cs = jnp.dot(lhs, m2, preferred_element_type=F32)
        run = run_ref[...]
        args = []
        for p in range(n):
            csp = cs[p * rows:(p + 1) * rows]
            args.append(blk(ls, p) + csp[:, :KEY_BLOCK] + run)
            run = run + csp[:, KEY_BLOCK:]
        w = jnp.exp2(jnp.concatenate(args, axis=1))
        if mask is not None:
            w = jnp.where(mask, w, 0.0)
        acc_ref[...] += (jnp.dot(w, vt, preferred_element_type=F32) if keys_major
                         else lax.dot_general(w, vt, nt, preferred_element_type=F32))
        run_ref[...] = run

    def start():
        acc_ref[...] = jnp.zeros_like(acc_ref)
        run_ref[...] = jnp.zeros_like(run_ref)
        key = lax.broadcasted_iota(jnp.int32, (rows, KEY_BLOCK), 1)
        qi = lax.broadcasted_iota(jnp.int32, (rows, KEY_BLOCK), 0) // N_HEADS
        n_new = knew_ref.shape[1]
        place = (lax.broadcasted_iota(jnp.int32, (KEY_BLOCK, n_new), 0)
                 == lax.broadcasted_iota(jnp.int32, (KEY_BLOCK, n_new), 1)).astype(F32)
        k_blk = jnp.dot(place, knew_ref[0], preferred_element_type=F32)
        v_blk = jnp.dot(place, vnew_ref[0], preferred_element_type=F32)
        attend(k_blk, v_blk, key < qi, keys_major=True)

    def pages():
        attend(jnp.concatenate([r[...] for r in k_refs], axis=1),
               jnp.concatenate([r[...] for r in v_refs], axis=1), None)

    def finish():
        picked = jnp.where(head_sel, acc_ref[...], 0.0)
        out = [jnp.sum(picked[t * N_HEADS:(t + 1) * N_HEADS], axis=0, keepdims=True) for t in range(nq)]
        o_ref[0] = jnp.concatenate(out, axis=0)

    return start, pages, finish


def _pages_transposed(cache):
    _, n_pool, page, h, hd = cache.shape
    return jnp.transpose(cache, (0, 1, 3, 4, 2)).reshape(n_pool, h * hd, page)


def _step_conv_kernel(hist_ref, u_ref, w_ref, b_ref, g_ref, beta_ref, s_ref):
    n_hist = hist_ref.shape[0]
    for t in range(u_ref.shape[0]):
        acc = jnp.broadcast_to(b_ref[...], u_ref.shape[1:])
        for tap in range(CONV_W):
            r = t + tap + n_hist - (CONV_W - 1)
            rows = hist_ref[r] if r < n_hist else u_ref[r - n_hist]
            acc = acc + w_ref[tap:tap + 1, :] * rows
        y = _ln(acc) * g_ref[...] + beta_ref[...]
        s_ref[t] = _silu(y)


def _step_conv_module(u_t, hist_t, conv_w, conv_b, cln_g, cln_b):
    c = u_t.shape[-1]
    return pl.pallas_call(
        _step_conv_kernel,
        out_shape=jax.ShapeDtypeStruct(u_t.shape, F32),
        compiler_params=pltpu.CompilerParams(vmem_limit_bytes=VMEM_LIMIT),
        name="step_conv_module",
    )(hist_t, u_t, conv_w, conv_b.reshape(1, c), cln_g.reshape(1, c), cln_b.reshape(1, c))


def _post_phases(phase, x_ref, attn_ref, s_ref, gate1_ref, shift2_ref, scale2_ref, gate2_ref,
                 wo_ref, g1_ref, b1_ref, wgu_ref, wdown_ref, g2_ref, b2_ref, y_ref, x1_ref, h2_ref, act_ref,
                 alongside=None):
    c = attn_ref.shape[1]
    d_ff = wdown_ref.shape[0]
    fc = d_ff // 2

    def out_proj():
        mix = jnp.dot(attn_ref[...].astype(BF16), wo_ref[0:c, :], preferred_element_type=F32)
        mix = mix + jnp.dot(s_ref[...].astype(BF16), wo_ref[c:, :], preferred_element_type=F32)
        x1 = _ln(ALPHA * x_ref[...] + (1.0 + gate1_ref[...]) * mix) * g1_ref[...] + b1_ref[...]
        x1_ref[...] = x1
        h2_ref[...] = (_ln(x1) * (1.0 + scale2_ref[...]) + shift2_ref[...]).astype(BF16)

    def swiglu(j):
        h2 = h2_ref[...]
        gt = jnp.dot(h2, wgu_ref[:, j * fc:(j + 1) * fc], preferred_element_type=F32)
        up = jnp.dot(h2, wgu_ref[:, d_ff + j * fc:d_ff + (j + 1) * fc], preferred_element_type=F32)
        act_ref[:, j * fc:(j + 1) * fc] = (_silu(gt) * up).astype(BF16)

    def down_proj():
        f = jnp.dot(act_ref[...], wdown_ref[...], preferred_element_type=F32)
        y_ref[...] = _ln(ALPHA * x1_ref[...] + (1.0 + gate2_ref[...]) * f) * g2_ref[...] + b2_ref[...]

    parts = (out_proj, functools.partial(swiglu, 0), functools.partial(swiglu, 1), down_proj)
    assert len(parts) == POST_PHASES
    for k, part in enumerate(parts):
        if phase is None:
            part()
        else:
            @pl.when(phase == k)
            def _(part=part):
                part()
                if alongside is not None:
                    alongside()


def _post_kernel(*refs):
    _post_phases(None, *refs)


def _post_specs(mod, tm, rows_per_group, d, c, cc, d_ff, tile):
    tok = lambda w: pl.BlockSpec((tm, w), lambda i, *_: (tile(i), 0))
    in_specs = ([tok(d), tok(c), tok(cc)]
                + [_mod_spec(mod, tm, rows_per_group, ch, tile) for ch in (2, 3, 4, 5)]
                + [_const_spec((d, d)), _const_spec((1, d)), _const_spec((1, d)),
                   _const_spec((d, 2 * d_ff)), _const_spec((d_ff, d)), _const_spec((1, d)), _const_spec((1, d))])
    scratch = [pltpu.VMEM((tm, d), F32), pltpu.VMEM((tm, d), BF16), pltpu.VMEM((tm, d_ff), BF16)]
    return in_specs, tok(d), scratch


def _post(x, attn, s, mod, rows_per_group, w_o, ln1_g, ln1_b, w_gu, w_down, ln2_g, ln2_b, tm):
    n, d = x.shape
    d_ff = w_down.shape[0]
    vec = lambda a: a.reshape(1, d)
    in_specs, out_spec, scratch = _post_specs(mod, tm, rows_per_group, d, attn.shape[1], s.shape[1], d_ff, lambda i: i)
    return pl.pallas_call(
        _post_kernel,
        grid=(n // tm,),
        in_specs=in_specs,
        out_specs=out_spec,
        out_shape=jax.ShapeDtypeStruct((n, d), F32),
        scratch_shapes=scratch,
        compiler_params=pltpu.CompilerParams(dimension_semantics=("parallel",),
                                             vmem_limit_bytes=VMEM_LIMIT),
        name="post",
    )(x, attn, s, mod, mod, mod, mod, w_o, vec(ln1_g), vec(ln1_b), w_gu, w_down, vec(ln2_g), vec(ln2_b))


N_POST_INPUTS = 14


def _post_sample_kernel(pages_per_step, steps_per_seq, pt_ref, *refs):
    post_in = refs[:N_POST_INPUTS]
    q_ref, bias_ref, knew_ref, vnew_ref, m2_ref, kcache_ref, vcache_ref = refs[N_POST_INPUTS:N_POST_INPUTS + 7]
    y_ref, o_ref, x1_ref, h2_ref, act_ref, acc_ref, run_ref, kbuf, vbuf, sems = refs[N_POST_INPUTS + 7:]
    i = pl.program_id(0)
    n_steps = pl.num_programs(0)
    n_pages = steps_per_seq * pages_per_step
    step = i % steps_per_seq
    slot = i % PAGE_SLOTS

    def page_copies(j):
        seq = j // steps_per_seq
        latest = n_pages - 1 - (j % steps_per_seq) * pages_per_step
        dst = j % PAGE_SLOTS
        copies = []
        for p in range(pages_per_step):
            page = pt_ref[seq, latest - p]
            copies.append(pltpu.make_async_copy(kcache_ref.at[page], kbuf.at[dst, p], sems.at[0, dst]))
            copies.append(pltpu.make_async_copy(vcache_ref.at[page], vbuf.at[dst, p], sems.at[1, dst]))
        return copies

    ahead = PAGE_SLOTS - 1

    @pl.when(i == 0)
    def _():
        for j in range(ahead):
            for copy in page_copies(j):
                copy.start()

    @pl.when(i + ahead < n_steps)
    def _():
        for copy in page_copies(i + ahead):
            copy.start()

    for copy in page_copies(i):
        copy.wait()

    k_pages = [kbuf.at[slot, p] for p in range(pages_per_step)]
    v_pages = [vbuf.at[slot, p] for p in range(pages_per_step)]
    start, attend_pages, finish = _sample_attn_parts(q_ref, bias_ref, knew_ref, vnew_ref, m2_ref,
                                                     k_pages, v_pages, o_ref, acc_ref, run_ref)
    pl.when(step == 0)(start)
    _post_phases(i % POST_PHASES, *post_in, y_ref, x1_ref, h2_ref, act_ref, alongside=attend_pages)
    pl.when(step == steps_per_seq - 1)(finish)


def _post_and_sample_attn(x, attn, s, mod, rows_per_group, w_o, ln1_g, ln1_b, w_gu, w_down, ln2_g, ln2_b, tm,
                          q, k_new, v_new, cache_k, cache_v, page_table, sb_bias, m2, pages_per_step=16):
    n, d = x.shape
    d_ff = w_down.shape[0]
    b, nq, c = q.shape
    n_pages = page_table.shape[1]
    steps_per_seq = n_pages // pages_per_step
    n_steps = b * steps_per_seq
    assert n_steps == (n // tm) * POST_PHASES, "token tiles and page steps must pair up"
    rows = nq * N_HEADS
    bias = jnp.broadcast_to(jnp.tile(sb_bias, nq)[:, None], (rows, KEY_BLOCK)).astype(F32)
    pad = ((0, 0), (0, -nq % SUBLANES), (0, 0))
    k_new = jnp.pad(k_new, pad)
    v_new = jnp.pad(v_new, pad)
    n_new = k_new.shape[1]

    per_seq = lambda shape: pl.BlockSpec(shape, lambda i, pt: (i // steps_per_seq, 0, 0))
    post_specs, y_spec, post_scratch = _post_specs(mod, tm, rows_per_group, d, attn.shape[1], s.shape[1], d_ff,
                                                   lambda i: i // POST_PHASES)
    page_buf = pltpu.VMEM((PAGE_SLOTS, pages_per_step, c, KEY_BLOCK), cache_k.dtype)
    grid_spec = pltpu.PrefetchScalarGridSpec(
        num_scalar_prefetch=1,
        grid=(n_steps,),
        in_specs=post_specs
                 + [per_seq((1, nq, c)), pl.BlockSpec((rows, KEY_BLOCK), lambda i, pt: (0, 0)),
                    per_seq((1, n_new, c)), per_seq((1, n_new, c)),
                    pl.BlockSpec(m2.shape, lambda i, pt: (0, 0)),
                    pl.BlockSpec(memory_space=pl.ANY), pl.BlockSpec(memory_space=pl.ANY)],
        out_specs=[y_spec, per_seq((1, nq, c))],
        scratch_shapes=post_scratch + [pltpu.VMEM((rows, c), F32), pltpu.VMEM((rows, KEY_BLOCK), F32),
                                       page_buf, page_buf, pltpu.SemaphoreType.DMA((2, PAGE_SLOTS))],
    )
    vec = lambda a: a.reshape(1, d)
    return pl.pallas_call(
        functools.partial(_post_sample_kernel, pages_per_step, steps_per_seq),
        grid_spec=grid_spec,
        out_shape=[jax.ShapeDtypeStruct((n, d), F32), jax.ShapeDtypeStruct((b, nq, c), F32)],
        compiler_params=pltpu.CompilerParams(dimension_semantics=("arbitrary",), vmem_limit_bytes=VMEM_LIMIT),
        name="post_sample_attn",
    )(page_table, x, attn, s, mod, mod, mod, mod, w_o, vec(ln1_g), vec(ln1_b), w_gu, w_down, vec(ln2_g), vec(ln2_b),
      q, bias, k_new, v_new, m2, cache_k, cache_v)


def kernel(x_prompt, x_sample, c_prompt, c_sample, cache_k, cache_v, page_table, state_conv, w_ada, b_ada, w_in, sb_bias, conv_w, conv_b, cln_g, cln_b, w_o, ln1_g, ln1_b, w_gu, w_down, ln2_g, ln2_b):
    assert w_in.shape[0] == 1, "single-layer model"
    bp, tp, d = x_prompt.shape
    bs, ts, _ = x_sample.shape
    c = C_ATTN
    cc = state_conv.shape[-1]
    hist = CONV_W - 1

    w_in_b = w_in[0].astype(BF16)
    w_o_b = w_o[0].astype(BF16)
    w_gu_b = w_gu[0].astype(BF16)
    w_down_b = w_down[0].astype(BF16)
    m2 = _suffix_matrix()
    bias2 = sb_bias[0] * LOG2E

    mod = _ada(jnp.concatenate([c_prompt, c_sample], axis=0), w_ada[0], b_ada[0])
    mod_p = mod[:bp].reshape(bp, 1, 6 * d)
    mod_s = jnp.repeat(mod[bp:], ts, axis=0).reshape(1, bs * ts, 6 * d)

    xp = x_prompt.reshape(bp * tp, d)
    q_p, kt_p, vt_p, kb_p, vb_p, s_p, u_tail = _inproj_seq(xp, mod_p, tp, w_in_b, conv_w[0], conv_b[0],
                                                           cln_g[0], cln_b[0], tm=512)
    attn_p = _prompt_attn(q_p.reshape(bp, tp, c), kb_p.reshape(bp, tp, c), vb_p.reshape(bp, tp, c),
                          -bias2, _pair_suffix_matrix())

    xs = x_sample.reshape(bs * ts, d)
    q_s, k_s, v_s, u_s = _inproj_step(xs, mod_s, ts, w_in_b)
    y_p, attn_s = _post_and_sample_attn(
        xp, attn_p.reshape(bp * tp, c), s_p, mod_p, tp,
        w_o_b, ln1_g[0], ln1_b[0], w_gu_b, w_down_b, ln2_g[0], ln2_b[0], 256,
        q_s.reshape(bs, ts, c), k_s.reshape(bs, ts, c), v_s.reshape(bs, ts, c),
        _pages_transposed(cache_k), _pages_transposed(cache_v), page_table, bias2, m2)
    u_st = jnp.swapaxes(u_s.reshape(bs, ts, cc), 0, 1)
    hist_t = jnp.swapaxes(state_conv[0], 0, 1)
    s_s = jnp.swapaxes(_step_conv_module(u_st, hist_t, conv_w[0], conv_b[0], cln_g[0], cln_b[0]), 0, 1)
    y_s = _post(xs, attn_s.reshape(bs * ts, c), s_s.reshape(bs * ts, cc), mod_s, ts,
                w_o_b, ln1_g[0], ln1_b[0], w_gu_b, w_down_b, ln2_g[0], ln2_b[0], tm=bs * ts)

    heads = lambda a, b, t: a.reshape(1, b, t, N_HEADS, HEAD_DIM)
    heads_t = lambda a: jnp.transpose(a.reshape(1, bp, N_HEADS, HEAD_DIM, tp), (0, 1, 4, 2, 3))
    new_conv_p = u_tail[:, HALO - hist:][None]
    new_conv_s = jnp.swapaxes(jnp.concatenate([hist_t[ts:], u_st], axis=0), 0, 1)[None]
    return (y_p.reshape(bp, tp, d), y_s.reshape(bs, ts, d),
            heads_t(kt_p), heads_t(vt_p), new_conv_p,
            heads(k_s, bs, ts), heads(v_s, bs, ts), new_conv_s)
```

```python
import functools

import jax
import jax.numpy as jnp
from jax import lax
from jax.experimental import pallas as pl
from jax.experimental.pallas import tpu as pltpu

F32 = jnp.float32
BF16 = jnp.bfloat16

HEAD_DIM = 64
N_HEADS = 8
C_ATTN = N_HEADS * HEAD_DIM
CONV_W = 31
SUBLANES = 8
HALO = 32
CONV_ROWS = 512
LN_EPS = 1e-5
ALPHA = 2.0 ** 0.25
KEY_BLOCK = 128
LOG2E = 1.4426950408889634
Q_SCALE = HEAD_DIM ** -0.5 * LOG2E
MAX_NEG_LOGIT = 126.0
Q_TILE = 512
HEAD_PAIRS = 4
LOOP_BLOCKS = 4
POST_PHASES = 4
PAGE_SLOTS = 3
VMEM_LIMIT = 56 * 1024 * 1024


def _ln(x):
    mu = jnp.mean(x, axis=-1, keepdims=True)
    xc = x - mu
    var = jnp.mean(xc * xc, axis=-1, keepdims=True)
    return xc * lax.rsqrt(var + LN_EPS)


def _silu(x):
    return x * jax.nn.sigmoid(x)


def _stick_logs(z2):
    ls = jnp.minimum(z2, 0.0) - jnp.log2(1.0 + jnp.exp2(-jnp.abs(z2)))
    return ls, ls - z2


def _stick_logs_neg(zn):
    zc = jnp.minimum(zn, MAX_NEG_LOGIT)
    return zc, zc - jnp.log2(1.0 + jnp.exp2(zc))


def _pair_suffix_matrix():
    r = lax.broadcasted_iota(jnp.int32, (2 * KEY_BLOCK, 2 * KEY_BLOCK), 0)
    c = lax.broadcasted_iota(jnp.int32, (2 * KEY_BLOCK, 2 * KEY_BLOCK), 1)
    return jnp.where((r // KEY_BLOCK == c // KEY_BLOCK) & (r >= c), 1.0, 0.0).astype(BF16)


def _suffix_matrix():
    r = lax.broadcasted_iota(jnp.int32, (KEY_BLOCK, 2 * KEY_BLOCK), 0)
    c = lax.broadcasted_iota(jnp.int32, (KEY_BLOCK, 2 * KEY_BLOCK), 1)
    return jnp.where((c >= KEY_BLOCK) | (r > c), 1.0, 0.0).astype(BF16)


def _ada_kernel(c_ref, w_ref, b_ref, o_ref):
    h = _silu(c_ref[...])
    o_ref[...] = jnp.dot(h, w_ref[...], preferred_element_type=F32) + b_ref[...]


def _ada(c, w_ada, b_ada):
    n, d = c.shape
    dn = w_ada.shape[1]
    tn = 2048
    return pl.pallas_call(
        _ada_kernel,
        grid=(dn // tn,),
        in_specs=[pl.BlockSpec((n, d), lambda j: (0, 0)),
                  pl.BlockSpec((d, tn), lambda j: (0, j)),
                  pl.BlockSpec((1, tn), lambda j: (0, j))],
        out_specs=pl.BlockSpec((n, tn), lambda j: (0, j)),
        out_shape=jax.ShapeDtypeStruct((n, dn), F32),
        compiler_params=pltpu.CompilerParams(dimension_semantics=("arbitrary",),
                                             vmem_limit_bytes=VMEM_LIMIT),
        name="ada",
    )(c, w_ada, b_ada.reshape(1, dn))


def _project(x_ref, shift_ref, scale_ref, w_ref, q_scale):
    c = C_ATTN
    cc = (w_ref.shape[1] - 3 * c) // 2
    h = (_ln(x_ref[...]) * (1.0 + scale_ref[...]) + shift_ref[...]).astype(BF16)
    dot = lambda lo, hi: jnp.dot(h, w_ref[:, lo:hi], preferred_element_type=F32)
    q = dot(0, c) * q_scale
    k = dot(c, 2 * c)
    v = dot(2 * c, 3 * c)
    u = dot(3 * c, 3 * c + cc) * jax.nn.sigmoid(dot(3 * c + cc, 3 * c + 2 * cc))
    return q, k, v, u


def _inproj_step_kernel(x_ref, shift_ref, scale_ref, w_ref, q_ref, k_ref, v_ref, u_ref):
    q, k, v, u = _project(x_ref, shift_ref, scale_ref, w_ref, Q_SCALE)
    q_ref[...] = q.astype(BF16)
    k_ref[...] = k
    v_ref[...] = v
    u_ref[...] = u


def _conv_rows(win_ref, phase_ref, w_ref, b_ref, g_ref, beta_ref, s_ref, tm):
    n = HALO + tm - SUBLANES
    for r in range(1, SUBLANES):
        phase_ref[r - 1, 0:n, :] = win_ref[r:r + n, :]

    def tap_rows(start):
        r = start % SUBLANES
        if r == 0:
            return win_ref[start:start + CONV_ROWS, :]
        return phase_ref[r - 1, start - r:start - r + CONV_ROWS, :]

    lead = HALO - (CONV_W - 1)
    for r0 in range(0, tm, CONV_ROWS):
        acc = jnp.broadcast_to(b_ref[...], (CONV_ROWS, b_ref.shape[1]))
        for tap in range(CONV_W):
            acc = acc + w_ref[tap:tap + 1, :] * tap_rows(r0 + lead + tap)
        y = _ln(acc) * g_ref[...] + beta_ref[...]
        s_ref[r0:r0 + CONV_ROWS, :] = _silu(y).astype(s_ref.dtype)


def _inproj_seq_kernel(tiles_per_seq, x_ref, shift_ref, scale_ref, w_ref, cw_ref, cb_ref, cg_ref, cbeta_ref,
                       q_ref, kt_ref, vt_ref, kb_ref, vb_ref, s_ref, tail_ref, win_ref, phase_ref):
    i = pl.program_id(0)
    tm = x_ref.shape[0]

    @pl.when(i == 0)
    def _():
        win_ref[...] = jnp.zeros_like(win_ref)

    q, k, v, u = _project(x_ref, shift_ref, scale_ref, w_ref, -Q_SCALE)
    q_ref[...] = q.astype(BF16)
    kb_ref[...] = k.astype(BF16)
    vb_ref[...] = v.astype(BF16)
    kt_ref[0] = k.T
    vt_ref[0] = v.T

    _conv_rows(win_ref, phase_ref, cw_ref, cb_ref, cg_ref, cbeta_ref, s_ref, tm)
    tail = win_ref[tm:tm + HALO, :]
    tail_ref[0] = tail
    win_ref[0:HALO, :] = jnp.where(i % tiles_per_seq == 0, 0.0, tail)
    win_ref[HALO:HALO + tm, :] = u


def _mod_spec(mod, tm, rows_per_group, chunk, tile=lambda i: i):
    d = mod.shape[-1] // 6
    if mod.shape[1] == 1:
        return pl.BlockSpec((None, 1, d), lambda i, *_: ((tile(i) * tm) // rows_per_group, 0, chunk))
    return pl.BlockSpec((None, tm, d), lambda i, *_: (0, tile(i), chunk))


def _const_spec(shape):
    return pl.BlockSpec(shape, lambda *_: (0,) * len(shape), pipeline_mode=pl.Buffered(1))


def _inproj_step(x, mod, rows_per_group, w_in):
    n, d = x.shape
    c = C_ATTN
    cc = (w_in.shape[1] - 3 * c) // 2
    tok = lambda w: pl.BlockSpec((n, w), lambda i: (0, 0))
    sds = jax.ShapeDtypeStruct
    return pl.pallas_call(
        _inproj_step_kernel,
        grid=(1,),
        in_specs=[tok(d), _mod_spec(mod, n, rows_per_group, 0), _mod_spec(mod, n, rows_per_group, 1),
                  _const_spec((d, w_in.shape[1]))],
        out_specs=[tok(c), tok(c), tok(c), tok(cc)],
        out_shape=[sds((n, c), BF16), sds((n, c), F32), sds((n, c), F32), sds((n, cc), F32)],
        compiler_params=pltpu.CompilerParams(dimension_semantics=("arbitrary",), vmem_limit_bytes=VMEM_LIMIT),
        name="inproj_step",
    )(x, mod, mod, w_in)


def _inproj_seq(x, mod, rows_per_group, w_in, conv_w, conv_b, cln_g, cln_b, tm):
    n, d = x.shape
    c = C_ATTN
    cc = (w_in.shape[1] - 3 * c) // 2
    assert rows_per_group % tm == 0 and tm % CONV_ROWS == 0 and tm >= HALO
    tiles = rows_per_group // tm
    n_tiles = n // tm
    n_seq = n // rows_per_group
    cur = lambda i: jnp.minimum(i, n_tiles - 1)
    prev = lambda i: jnp.maximum(i - 1, 0)
    tok = lambda w, tile: pl.BlockSpec((tm, w), lambda i: (tile(i), 0))
    kt_spec = pl.BlockSpec((1, c, tm), lambda i: (cur(i) // tiles, 0, cur(i) % tiles))
    vec = _const_spec((1, cc))
    sds = jax.ShapeDtypeStruct
    return pl.pallas_call(
        functools.partial(_inproj_seq_kernel, tiles),
        grid=(n_tiles + 1,),
        in_specs=[tok(d, cur), _mod_spec(mod, tm, rows_per_group, 0, cur), _mod_spec(mod, tm, rows_per_group, 1, cur),
                  _const_spec((d, w_in.shape[1])), _const_spec((CONV_W, cc)), vec, vec, vec],
        out_specs=[tok(c, cur), kt_spec, kt_spec, tok(c, cur), tok(c, cur), tok(cc, prev),
                   pl.BlockSpec((1, HALO, cc), lambda i: (prev(i) // tiles, 0, 0))],
        out_shape=[sds((n, c), BF16), sds((n_seq, c, rows_per_group), F32), sds((n_seq, c, rows_per_group), F32),
                   sds((n, c), BF16), sds((n, c), BF16), sds((n, cc), BF16), sds((n_seq, HALO, cc), F32)],
        scratch_shapes=[pltpu.VMEM((HALO + tm, cc), F32), pltpu.VMEM((SUBLANES - 1, HALO + tm, cc), F32)],
        compiler_params=pltpu.CompilerParams(dimension_semantics=("arbitrary",), vmem_limit_bytes=VMEM_LIMIT),
        name="inproj_seq",
    )(x, mod, mod, w_in, conv_w, conv_b.reshape(1, cc), cln_g.reshape(1, cc), cln_b.reshape(1, cc))


def _prompt_attn_kernel(bias_ref, q_ref, k_ref, v_ref, m2_ref, o_ref, acc_ref, run_ref):
    hg = pl.program_id(1)
    i = pl.program_id(2)
    tq = q_ref.shape[1]
    kb_per_tile = tq // KEY_BLOCK
    pw = 2 * HEAD_DIM
    m2 = m2_ref[...]
    lane = lax.broadcasted_iota(jnp.int32, (KEY_BLOCK, pw), 1)
    head_a = lane < HEAD_DIM
    col2 = lax.broadcasted_iota(jnp.int32, (1, 2 * KEY_BLOCK), 1)
    bias2 = [jnp.where(col2 < KEY_BLOCK, bias_ref[2 * (HEAD_PAIRS * hg + p)], bias_ref[2 * (HEAD_PAIRS * hg + p) + 1])
             for p in range(HEAD_PAIRS)]

    def block(j, r0, masked):
        start = pl.multiple_of(j * KEY_BLOCK, KEY_BLOCK)
        for p in range(HEAD_PAIRS):
            lanes = slice(p * pw, (p + 1) * pw)
            kb = k_ref[0, pl.ds(start, KEY_BLOCK), lanes]
            vb = v_ref[0, pl.ds(start, KEY_BLOCK), lanes]
            zero = jnp.zeros_like(kb)
            k2 = jnp.concatenate([jnp.where(head_a, kb, zero), jnp.where(head_a, zero, kb)], axis=0)
            v2 = jnp.concatenate([jnp.where(head_a, vb, zero), jnp.where(head_a, zero, vb)], axis=0)
            z = lax.dot_general(q_ref[0, r0:, lanes], k2, (((1,), (1,)), ((), ())),
                                preferred_element_type=F32) + bias2[p]
            zc, lom = _stick_logs_neg(z)
            if masked:
                n = tq - r0
                key = lax.broadcasted_iota(jnp.int32, (n, 2 * KEY_BLOCK), 1) % KEY_BLOCK
                qrow = lax.broadcasted_iota(jnp.int32, (n, 2 * KEY_BLOCK), 0)
                mask = key < qrow
                lom = jnp.where(mask, lom, 0.0)
            incl = jnp.dot(lom.astype(BF16), m2, preferred_element_type=F32)
            n_rows = incl.shape[0]
            total = jnp.concatenate([jnp.broadcast_to(incl[:, h * KEY_BLOCK:h * KEY_BLOCK + 1], (n_rows, KEY_BLOCK))
                                     for h in range(2)], axis=1)
            w = jnp.exp2(incl + run_ref[p, r0:, :] - zc)
            if masked:
                w = jnp.where(mask, w, 0.0)
            acc_ref[p, r0:, :] += jnp.dot(w.astype(BF16), v2, preferred_element_type=F32)
            run_ref[p, r0:, :] += total

    acc_ref[...] = jnp.zeros_like(acc_ref)
    run_ref[...] = jnp.zeros_like(run_ref)
    for jj in reversed(range(kb_per_tile)):
        block(i * kb_per_tile + jj, jj * KEY_BLOCK, True)

    def body(t, carry):
        for u in range(LOOP_BLOCKS):
            block(i * kb_per_tile - 1 - (t * LOOP_BLOCKS + u), 0, False)
        return carry

    lax.fori_loop(0, i * (kb_per_tile // LOOP_BLOCKS), body, 0)
    for p in range(HEAD_PAIRS):
        o_ref[0, :, p * pw:(p + 1) * pw] = acc_ref[p].astype(o_ref.dtype)


def _prompt_attn(q, k, v, neg_bias, m2):
    b, t, c = q.shape
    tq = Q_TILE
    hw = HEAD_PAIRS * 2 * HEAD_DIM
    assert t % tq == 0 and (tq // KEY_BLOCK) % LOOP_BLOCKS == 0 and c % hw == 0
    grid_spec = pltpu.PrefetchScalarGridSpec(
        num_scalar_prefetch=1,
        grid=(b, c // hw, t // tq),
        in_specs=[pl.BlockSpec((1, tq, hw), lambda bb, hg, i, bias: (bb, i, hg)),
                  pl.BlockSpec((1, t, hw), lambda bb, hg, i, bias: (bb, 0, hg)),
                  pl.BlockSpec((1, t, hw), lambda bb, hg, i, bias: (bb, 0, hg)),
                  pl.BlockSpec(m2.shape, lambda bb, hg, i, bias: (0, 0))],
        out_specs=pl.BlockSpec((1, tq, hw), lambda bb, hg, i, bias: (bb, i, hg)),
        scratch_shapes=[pltpu.VMEM((HEAD_PAIRS, tq, 2 * HEAD_DIM), F32),
                        pltpu.VMEM((HEAD_PAIRS, tq, 2 * KEY_BLOCK), F32)],
    )
    return pl.pallas_call(
        _prompt_attn_kernel,
        grid_spec=grid_spec,
        out_shape=jax.ShapeDtypeStruct((b, t, c), BF16),
        compiler_params=pltpu.CompilerParams(
            dimension_semantics=("parallel", "parallel", "arbitrary"), vmem_limit_bytes=VMEM_LIMIT),
        name="prompt_attn",
    )(neg_bias, q, k, v, m2)


def _sample_attn_parts(q_ref, bias_ref, knew_ref, vnew_ref, m2_ref, k_refs, v_refs, o_ref, acc_ref, run_ref, vkeys_ref):
    nq = q_ref.shape[1]
    rows = nq * N_HEADS
    c = q_ref.shape[2]
    row = lax.broadcasted_iota(jnp.int32, (rows, c), 0)
    colh = lax.broadcasted_iota(jnp.int32, (rows, c), 1) // HEAD_DIM
    q = q_ref[0].astype(F32)
    q_rows = jnp.concatenate([jnp.broadcast_to(q[t:t + 1], (N_HEADS, c)) for t in range(nq)], axis=0)
    head_sel = (row % N_HEADS) == colh
    qbd = jnp.where(head_sel, q_rows, 0.0)
    bias = bias_ref[...]
    m2 = m2_ref[...]

    nt = (((1,), (1,)), ((), ()))

    def attend(kt, vt, mask, keys_major=False):
        n = kt.shape[0 if keys_major else 1] // KEY_BLOCK
        blk = lambda a, p: a[:, p * KEY_BLOCK:(p + 1) * KEY_BLOCK]
        z = (lax.dot_general(qbd, kt, nt, preferred_element_type=F32) if keys_major
             else jnp.dot(qbd, kt, preferred_element_type=F32)) + jnp.concatenate([bias] * n, axis=1)
        ls, lom = _stick_logs(z)
        if mask is not None:
            lom = jnp.where(mask, lom, 0.0)
        lom16 = lom.astype(BF16)
        lhs = jnp.concatenate([blk(lom16, p) for p in range(n)], axis=0)
        cs = jnp.dot(lhs, m2, preferred_element_type=F32)
        run = run_ref[...]
        args = []
        for p in range(n):
            csp = cs[p * rows:(p + 1) * rows]
            args.append(blk(ls, p) + csp[:, :KEY_BLOCK] + run)
            run = run + csp[:, KEY_BLOCK:]
        w = jnp.exp2(jnp.concatenate(args, axis=1))
        if mask is not None:
            w = jnp.where(mask, w, 0.0)
        if keys_major:
            v_keys = vt
        else:
            vkeys_ref[...] = vt.astype(BF16).T
            v_keys = vkeys_ref[...]
        acc_ref[...] += jnp.dot(w.astype(v_keys.dtype), v_keys, preferred_element_type=F32)
        run_ref[...] = run

    def start():
        acc_ref[...] = jnp.zeros_like(acc_ref)
        run_ref[...] = jnp.zeros_like(run_ref)
        key = lax.broadcasted_iota(jnp.int32, (rows, KEY_BLOCK), 1)
        qi = lax.broadcasted_iota(jnp.int32, (rows, KEY_BLOCK), 0) // N_HEADS
        n_new = knew_ref.shape[1]
        place = (lax.broadcasted_iota(jnp.int32, (KEY_BLOCK, n_new), 0)
                 == lax.broadcasted_iota(jnp.int32, (KEY_BLOCK, n_new), 1)).astype(F32)
        k_blk = jnp.dot(place, knew_ref[0], preferred_element_type=F32)
        v_blk = jnp.dot(place, vnew_ref[0], preferred_element_type=F32)
        attend(k_blk, v_blk, key < qi, keys_major=True)

    def pages():
        attend(jnp.concatenate([r[...] for r in k_refs], axis=1),
               jnp.concatenate([r[...] for r in v_refs], axis=1), None)

    def finish():
        picked = jnp.where(head_sel, acc_ref[...], 0.0)
        out = [jnp.sum(picked[t * N_HEADS:(t + 1) * N_HEADS], axis=0, keepdims=True) for t in range(nq)]
        o_ref[0] = jnp.concatenate(out, axis=0)

    return start, pages, finish


def _pages_transposed(cache):
    _, n_pool, page, h, hd = cache.shape
    return jnp.transpose(cache, (0, 1, 3, 4, 2)).reshape(n_pool, h * hd, page)


def _step_conv_kernel(hist_ref, u_ref, w_ref, b_ref, g_ref, beta_ref, s_ref):
    n_hist = hist_ref.shape[0]
    for t in range(u_ref.shape[0]):
        acc = jnp.broadcast_to(b_ref[...], u_ref.shape[1:])
        for tap in range(CONV_W):
            r = t + tap + n_hist - (CONV_W - 1)
            rows = hist_ref[r] if r < n_hist else u_ref[r - n_hist]
            acc = acc + w_ref[tap:tap + 1, :] * rows
        y = _ln(acc) * g_ref[...] + beta_ref[...]
        s_ref[t] = _silu(y)


def _step_conv_module(u_t, hist_t, conv_w, conv_b, cln_g, cln_b):
    c = u_t.shape[-1]
    return pl.pallas_call(
        _step_conv_kernel,
        out_shape=jax.ShapeDtypeStruct(u_t.shape, F32),
        compiler_params=pltpu.CompilerParams(vmem_limit_bytes=VMEM_LIMIT),
        name="step_conv_module",
    )(hist_t, u_t, conv_w, conv_b.reshape(1, c), cln_g.reshape(1, c), cln_b.reshape(1, c))


def _post_phases(phase, x_ref, attn_ref, s_ref, gate1_ref, shift2_ref, scale2_ref, gate2_ref,
                 wo_ref, g1_ref, b1_ref, wgu_ref, wdown_ref, g2_ref, b2_ref, y_ref, x1_ref, h2_ref, act_ref,
                 alongside=None):
    c = attn_ref.shape[1]
    d_ff = wdown_ref.shape[0]
    fc = d_ff // 2

    def out_proj():
        mix = jnp.dot(attn_ref[...].astype(BF16), wo_ref[0:c, :], preferred_element_type=F32)
        mix = mix + jnp.dot(s_ref[...].astype(BF16), wo_ref[c:, :], preferred_element_type=F32)
        x1 = _ln(ALPHA * x_ref[...] + (1.0 + gate1_ref[...]) * mix) * g1_ref[...] + b1_ref[...]
        x1_ref[...] = x1
        h2_ref[...] = (_ln(x1) * (1.0 + scale2_ref[...]) + shift2_ref[...]).astype(BF16)

    def swiglu(j):
        h2 = h2_ref[...]
        gt = jnp.dot(h2, wgu_ref[:, j * fc:(j + 1) * fc], preferred_element_type=F32)
        up = jnp.dot(h2, wgu_ref[:, d_ff + j * fc:d_ff + (j + 1) * fc], preferred_element_type=F32)
        act_ref[:, j * fc:(j + 1) * fc] = (_silu(gt) * up).astype(BF16)

    def down_proj():
        f = jnp.dot(act_ref[...], wdown_ref[...], preferred_element_type=F32)
        y_ref[...] = _ln(ALPHA * x1_ref[...] + (1.0 + gate2_ref[...]) * f) * g2_ref[...] + b2_ref[...]

    parts = (out_proj, functools.partial(swiglu, 0), functools.partial(swiglu, 1), down_proj)
    assert len(parts) == POST_PHASES
    for k, part in enumerate(parts):
        if phase is None:
            part()
        else:
            @pl.when(phase == k)
            def _(part=part):
                part()
                if alongside is not None:
                    alongside()


def _post_kernel(*refs):
    _post_phases(None, *refs)


def _post_specs(mod, tm, rows_per_group, d, c, cc, d_ff, tile):
    tok = lambda w: pl.BlockSpec((tm, w), lambda i, *_: (tile(i), 0))
    in_specs = ([tok(d), tok(c), tok(cc)]
                + [_mod_spec(mod, tm, rows_per_group, ch, tile) for ch in (2, 3, 4, 5)]
                + [_const_spec((d, d)), _const_spec((1, d)), _const_spec((1, d)),
                   _const_spec((d, 2 * d_ff)), _const_spec((d_ff, d)), _const_spec((1, d)), _const_spec((1, d))])
    scratch = [pltpu.VMEM((tm, d), F32), pltpu.VMEM((tm, d), BF16), pltpu.VMEM((tm, d_ff), BF16)]
    return in_specs, tok(d), scratch


def _post(x, attn, s, mod, rows_per_group, w_o, ln1_g, ln1_b, w_gu, w_down, ln2_g, ln2_b, tm):
    n, d = x.shape
    d_ff = w_down.shape[0]
    vec = lambda a: a.reshape(1, d)
    in_specs, out_spec, scratch = _post_specs(mod, tm, rows_per_group, d, attn.shape[1], s.shape[1], d_ff, lambda i: i)
    return pl.pallas_call(
        _post_kernel,
        grid=(n // tm,),
        in_specs=in_specs,
        out_specs=out_spec,
        out_shape=jax.ShapeDtypeStruct((n, d), F32),
        scratch_shapes=scratch,
        compiler_params=pltpu.CompilerParams(dimension_semantics=("parallel",),
                                             vmem_limit_bytes=VMEM_LIMIT),
        name="post",
    )(x, attn, s, mod, mod, mod, mod, w_o, vec(ln1_g), vec(ln1_b), w_gu, w_down, vec(ln2_g), vec(ln2_b))


N_POST_INPUTS = 14


def _post_sample_kernel(pages_per_step, steps_per_seq, pt_ref, *refs):
    post_in = refs[:N_POST_INPUTS]
    q_ref, bias_ref, knew_ref, vnew_ref, m2_ref, kcache_ref, vcache_ref = refs[N_POST_INPUTS:N_POST_INPUTS + 7]
    y_ref, o_ref, x1_ref, h2_ref, act_ref, acc_ref, run_ref, kbuf, vbuf, sems, vkeys_ref = refs[N_POST_INPUTS + 7:]
    i = pl.program_id(0)
    n_steps = pl.num_programs(0)
    n_pages = steps_per_seq * pages_per_step
    step = i % steps_per_seq
    slot = i % PAGE_SLOTS

    def page_copies(j):
        seq = j // steps_per_seq
        latest = n_pages - 1 - (j % steps_per_seq) * pages_per_step
        dst = j % PAGE_SLOTS
        copies = []
        for p in range(pages_per_step):
            page = pt_ref[seq, latest - p]
            copies.append(pltpu.make_async_copy(kcache_ref.at[page], kbuf.at[dst, p], sems.at[0, dst]))
            copies.append(pltpu.make_async_copy(vcache_ref.at[page], vbuf.at[dst, p], sems.at[1, dst]))
        return copies

    ahead = PAGE_SLOTS - 1

    @pl.when(i == 0)
    def _():
        for j in range(ahead):
            for copy in page_copies(j):
                copy.start()

    @pl.when(i + ahead < n_steps)
    def _():
        for copy in page_copies(i + ahead):
            copy.start()

    for copy in page_copies(i):
        copy.wait()

    k_pages = [kbuf.at[slot, p] for p in range(pages_per_step)]
    v_pages = [vbuf.at[slot, p] for p in range(pages_per_step)]
    start, attend_pages, finish = _sample_attn_parts(q_ref, bias_ref, knew_ref, vnew_ref, m2_ref,
                                                     k_pages, v_pages, o_ref, acc_ref, run_ref, vkeys_ref)
    pl.when(step == 0)(start)
    _post_phases(i % POST_PHASES, *post_in, y_ref, x1_ref, h2_ref, act_ref, alongside=attend_pages)
    pl.when(step == steps_per_seq - 1)(finish)


def _post_and_sample_attn(x, attn, s, mod, rows_per_group, w_o, ln1_g, ln1_b, w_gu, w_down, ln2_g, ln2_b, tm,
                          q, k_new, v_new, cache_k, cache_v, page_table, sb_bias, m2, pages_per_step=16):
    n, d = x.shape
    d_ff = w_down.shape[0]
    b, nq, c = q.shape
    n_pages = page_table.shape[1]
    steps_per_seq = n_pages // pages_per_step
    n_steps = b * steps_per_seq
    assert n_steps == (n // tm) * POST_PHASES, "token tiles and page steps must pair up"
    rows = nq * N_HEADS
    bias = jnp.broadcast_to(jnp.tile(sb_bias, nq)[:, None], (rows, KEY_BLOCK)).astype(F32)
    pad = ((0, 0), (0, -nq % SUBLANES), (0, 0))
    k_new = jnp.pad(k_new, pad)
    v_new = jnp.pad(v_new, pad)
    n_new = k_new.shape[1]

    per_seq = lambda shape: pl.BlockSpec(shape, lambda i, pt: (i // steps_per_seq, 0, 0))
    post_specs, y_spec, post_scratch = _post_specs(mod, tm, rows_per_group, d, attn.shape[1], s.shape[1], d_ff,
                                                   lambda i: i // POST_PHASES)
    page_buf = pltpu.VMEM((PAGE_SLOTS, pages_per_step, c, KEY_BLOCK), cache_k.dtype)
    grid_spec = pltpu.PrefetchScalarGridSpec(
        num_scalar_prefetch=1,
        grid=(n_steps,),
        in_specs=post_specs
                 + [per_seq((1, nq, c)), pl.BlockSpec((rows, KEY_BLOCK), lambda i, pt: (0, 0)),
                    per_seq((1, n_new, c)), per_seq((1, n_new, c)),
                    pl.BlockSpec(m2.shape, lambda i, pt: (0, 0)),
                    pl.BlockSpec(memory_space=pl.ANY), pl.BlockSpec(memory_space=pl.ANY)],
        out_specs=[y_spec, per_seq((1, nq, c))],
        scratch_shapes=post_scratch + [pltpu.VMEM((rows, c), F32), pltpu.VMEM((rows, KEY_BLOCK), F32),
                                       page_buf, page_buf, pltpu.SemaphoreType.DMA((2, PAGE_SLOTS)),
                                       pltpu.VMEM((pages_per_step * KEY_BLOCK, c), BF16)],
    )
    vec = lambda a: a.reshape(1, d)
    return pl.pallas_call(
        functools.partial(_post_sample_kernel, pages_per_step, steps_per_seq),
        grid_spec=grid_spec,
        out_shape=[jax.ShapeDtypeStruct((n, d), F32), jax.ShapeDtypeStruct((b, nq, c), F32)],
        compiler_params=pltpu.CompilerParams(dimension_semantics=("arbitrary",), vmem_limit_bytes=VMEM_LIMIT),
        name="post_sample_attn",
    )(page_table, x, attn, s, mod, mod, mod, mod, w_o, vec(ln1_g), vec(ln1_b), w_gu, w_down, vec(ln2_g), vec(ln2_b),
      q, bias, k_new, v_new, m2, cache_k, cache_v)


def kernel(x_prompt, x_sample, c_prompt, c_sample, cache_k, cache_v, page_table, state_conv, w_ada, b_ada, w_in, sb_bias, conv_w, conv_b, cln_g, cln_b, w_o, ln1_g, ln1_b, w_gu, w_down, ln2_g, ln2_b):
    assert w_in.shape[0] == 1, "single-layer model"
    bp, tp, d = x_prompt.shape
    bs, ts, _ = x_sample.shape
    c = C_ATTN
    cc = state_conv.shape[-1]
    hist = CONV_W - 1

    w_in_b = w_in[0].astype(BF16)
    w_o_b = w_o[0].astype(BF16)
    w_gu_b = w_gu[0].astype(BF16)
    w_down_b = w_down[0].astype(BF16)
    m2 = _suffix_matrix()
    bias2 = sb_bias[0] * LOG2E

    mod = _ada(jnp.concatenate([c_prompt, c_sample], axis=0), w_ada[0], b_ada[0])
    mod_p = mod[:bp].reshape(bp, 1, 6 * d)
    mod_s = jnp.repeat(mod[bp:], ts, axis=0).reshape(1, bs * ts, 6 * d)

    xp = x_prompt.reshape(bp * tp, d)
    q_p, kt_p, vt_p, kb_p, vb_p, s_p, u_tail = _inproj_seq(xp, mod_p, tp, w_in_b, conv_w[0], conv_b[0],
                                                           cln_g[0], cln_b[0], tm=512)
    attn_p = _prompt_attn(q_p.reshape(bp, tp, c), kb_p.reshape(bp, tp, c), vb_p.reshape(bp, tp, c),
                          -bias2, _pair_suffix_matrix())

    xs = x_sample.reshape(bs * ts, d)
    q_s, k_s, v_s, u_s = _inproj_step(xs, mod_s, ts, w_in_b)
    y_p, attn_s = _post_and_sample_attn(
        xp, attn_p.reshape(bp * tp, c), s_p, mod_p, tp,
        w_o_b, ln1_g[0], ln1_b[0], w_gu_b, w_down_b, ln2_g[0], ln2_b[0], 256,
        q_s.reshape(bs, ts, c), k_s.reshape(bs, ts, c), v_s.reshape(bs, ts, c),
        _pages_transposed(cache_k), _pages_transposed(cache_v), page_table, bias2, m2)
    u_st = jnp.swapaxes(u_s.reshape(bs, ts, cc), 0, 1)
    hist_t = jnp.swapaxes(state_conv[0], 0, 1)
    s_s = jnp.swapaxes(_step_conv_module(u_st, hist_t, conv_w[0], conv_b[0], cln_g[0], cln_b[0]), 0, 1)
    y_s = _post(xs, attn_s.reshape(bs * ts, c), s_s.reshape(bs * ts, cc), mod_s, ts,
                w_o_b, ln1_g[0], ln1_b[0], w_gu_b, w_down_b, ln2_g[0], ln2_b[0], tm=bs * ts)

    heads = lambda a, b, t: a.reshape(1, b, t, N_HEADS, HEAD_DIM)
    heads_t = lambda a: jnp.transpose(a.reshape(1, bp, N_HEADS, HEAD_DIM, tp), (0, 1, 4, 2, 3))
    new_conv_p = u_tail[:, HALO - hist:][None]
    new_conv_s = jnp.swapaxes(jnp.concatenate([hist_t[ts:], u_st], axis=0), 0, 1)[None]
    return (y_p.reshape(bp, tp, d), y_s.reshape(bs, ts, d),
            heads_t(kt_p), heads_t(vt_p), new_conv_p,
            heads(k_s, bs, ts), heads(v_s, bs, ts), new_conv_s)
```
